```python
import math
import jax, jax.numpy as jnp
from jax import lax
import numpy as np

D_MODEL = 4096
BATCH = 4
SEQ = 2048
DEPTH = 4
DEC_BATCH = 128
DEC_SEQ = 1
PAST_LEN = 8192
PAGE_SIZE = 128

H_A = 12
H_B = 8
H_C = 24
QK_NOPE = 128
QK_ROPE = 64
V_HEAD = 128
KV_LORA = 256
IDX_HEADS = 32
IDX_DIM = 64
IDX_ROPE = 32
IDX_TOPK = 256
DIFF_D = 32
DIFF_V = 2 * DIFF_D
MIX_WIDTH = (H_A + H_B) * V_HEAD + H_C * DIFF_V
D_FF = 4 * D_MODEL
PLE_DIM = 256
ROPE_THETA = 10000.0
EPS = 1e-6
Q_BLOCK = 128
MLA_ROW = KV_LORA + QK_ROPE
DIFF_ROW = 2 * DIFF_D + DIFF_V
IN_COLS = (H_A * QK_NOPE, H_A * QK_ROPE, KV_LORA, QK_ROPE,
           H_B * QK_NOPE, H_B * QK_ROPE,
           IDX_HEADS * IDX_DIM, IDX_DIM, IDX_HEADS,
           H_C * 2 * DIFF_D, 2 * DIFF_D, DIFF_V)
IN_WIDTH = sum(IN_COLS)
IN_SPLITS = tuple(np.cumsum(IN_COLS)[:-1].tolist())

kernel_name = 'hybrid_mla_dsa_diff_decoder_step'


def _rms(x, g):
    xf = x.astype(jnp.float32)
    y = xf * lax.rsqrt(jnp.mean(xf * xf, axis=-1, keepdims=True) + EPS)
    return (y * g.astype(jnp.float32)).astype(x.dtype)


def _rope(x, pos):
    half = x.shape[-1] // 2
    inv = ROPE_THETA ** (-jnp.arange(half, dtype=jnp.float32) / half)
    ang = pos.astype(jnp.float32)[:, None] * inv[None, :]
    ang = ang.reshape((ang.shape[0],) + (1,) * (x.ndim - 3) + (half,))
    cos, sin = jnp.cos(ang), jnp.sin(ang)
    xf = x.astype(jnp.float32)
    x1, x2 = xf[..., :half], xf[..., half:]
    return jnp.concatenate([x1 * cos - x2 * sin, x2 * cos + x1 * sin], axis=-1).astype(x.dtype)


def _rope_partial(x, pos):
    return jnp.concatenate([_rope(x[..., :IDX_ROPE], pos), x[..., IDX_ROPE:]], axis=-1)


def _absorb(q_nope, q_rope, w_uk, pos):
    q_lat = jnp.einsum('bthd,chd->bthc', q_nope, w_uk)
    return jnp.concatenate([q_lat, _rope(q_rope, pos)], axis=-1)


def _map_query_blocks(fn, qs, q_pos):
    T = q_pos.shape[0]
    if T <= Q_BLOCK or T % Q_BLOCK:
        return fn(*qs, q_pos)
    nb = T // Q_BLOCK

    def split(a):
        return jnp.moveaxis(a.reshape((a.shape[0], nb, Q_BLOCK) + a.shape[2:]), 1, 0)

    blocks = [split(a) for a in qs] + [q_pos.reshape(nb, Q_BLOCK)]
    out = lax.map(lambda args: fn(*args), blocks)
    out = jnp.moveaxis(out, 0, 1)
    return out.reshape((out.shape[0], T) + out.shape[3:])


def _project(xn, pos, w_in, kv_norm, w_uk_a, w_uk_b):
    B, T, _ = xn.shape
    (qa_n, qa_r, c_kv, k_r, qb_n, qb_r, q_idx, k_idx, w_idx,
     q_d, k_d, v_d) = jnp.split(xn @ w_in, IN_SPLITS, axis=-1)
    mla_row = jnp.concatenate([_rms(c_kv, kv_norm), _rope(k_r, pos)], axis=-1)
    q_a = _absorb(qa_n.reshape(B, T, H_A, QK_NOPE), qa_r.reshape(B, T, H_A, QK_ROPE), w_uk_a, pos)
    q_b = _absorb(qb_n.reshape(B, T, H_B, QK_NOPE), qb_r.reshape(B, T, H_B, QK_ROPE), w_uk_b, pos)
    q_idx = _rope_partial(q_idx.reshape(B, T, IDX_HEADS, IDX_DIM), pos)
    idx_row = _rope_partial(k_idx, pos)
    w_idx = w_idx * (IDX_HEADS ** -0.5)
    q_d = _rope(q_d.reshape(B, T, H_C, 2, DIFF_D), pos)
    k_d = _rope(k_d.reshape(B, T, 2, DIFF_D), pos).reshape(B, T, 2 * DIFF_D)
    diff_row = jnp.concatenate([k_d, v_d], axis=-1)
    return q_a, q_b, q_idx, w_idx, q_d, mla_row, idx_row, diff_row


def _mla_attend(q, kv, q_pos, k_pos):
    scale = (QK_NOPE + QK_ROPE) ** -0.5

    def block(qb, pb):
        s = jnp.einsum('bthc,bsc->bhts', qb, kv).astype(jnp.float32) * scale
        s = jnp.where(k_pos[None, :] <= pb[:, None], s, -jnp.inf)
        w = jax.nn.softmax(s, axis=-1).astype(kv.dtype)
        return jnp.einsum('bhts,bsc->bthc', w, kv[..., :KV_LORA])

    return _map_query_blocks(block, (q,), q_pos)


def _dsa_attend(q, q_idx, w_idx, k_idx, kv, q_pos, k_pos):
    scale = (QK_NOPE + QK_ROPE) ** -0.5
    topk = min(IDX_TOPK, kv.shape[1] // 4)

    def block(qb, qib, wb, pb):
        rel = jax.nn.relu(jnp.einsum('bthd,bsd->bths', qib, k_idx).astype(jnp.float32))
        score = jnp.einsum('bths,bth->bts', rel, wb.astype(jnp.float32)) * (IDX_DIM ** -0.5)
        score = jnp.where(k_pos[None, :] <= pb[:, None], score, -jnp.inf)
        _, sel = lax.top_k(score, topk)
        ok = k_pos[sel] <= pb[None, :, None]
        rows = jax.vmap(lambda kv_b, sel_b: kv_b[sel_b])(kv, sel)
        s = jnp.einsum('bthc,btkc->bhtk', qb, rows).astype(jnp.float32) * scale
        s = jnp.where(ok[:, None], s, -jnp.inf)
        w = jax.nn.softmax(s, axis=-1).astype(rows.dtype)
        return jnp.einsum('bhtk,btkc->bthc', w, rows[..., :KV_LORA])

    return _map_query_blocks(block, (q, q_idx, w_idx), q_pos)


def _diff_attend(q, dkv, q_pos, k_pos, lam_vec, subln, layer):
    B, S, _ = dkv.shape
    k = dkv[..., :2 * DIFF_D].reshape(B, S, 2, DIFF_D)
    v = dkv[..., 2 * DIFF_D:]
    lam_init = 0.8 - 0.6 * math.exp(-0.3 * layer)
    lv = lam_vec.astype(jnp.float32)
    lam = jnp.exp(jnp.sum(lv[0] * lv[1])) - jnp.exp(jnp.sum(lv[2] * lv[3])) + lam_init
    scale = DIFF_D ** -0.5

    def block(qb, pb):
        s = jnp.einsum('bthnd,bsnd->bnhts', qb, k).astype(jnp.float32) * scale
        s = jnp.where(k_pos[None, :] <= pb[:, None], s, -jnp.inf)
        pr = jax.nn.softmax(s, axis=-1)
        a = pr[:, 0] - lam * pr[:, 1]
        return jnp.einsum('bhts,bsd->bthd', a.astype(v.dtype), v)

    o = _map_query_blocks(block, (q,), q_pos)
    return _rms(o, subln) * (1.0 - lam_init)


def _layer_stack(x, p, q_pos, past_fn, norm_attn, w_in, kv_norm, w_uk_a, w_uv_a, w_uk_b, w_uv_b,
                 diff_lambda, diff_subln, w_out, norm_mlp, w_up, w_down, norm_ple, w_ple_gate,
                 w_ple_proj, norm_final):
    B, T, _ = x.shape
    h = x
    mla_rows, idx_rows, diff_rows = [], [], []
    for i in range(DEPTH):
        xn = _rms(h, norm_attn[i])
        q_a, q_b, q_idx, w_idx, q_d, mla_row, idx_row, diff_row = _project(
            xn, q_pos, w_in[i], kv_norm[i], w_uk_a[i], w_uk_b[i])
        mla_rows.append(mla_row)
        idx_rows.append(idx_row)
        diff_rows.append(diff_row)
        kv, k_idx, dkv, k_pos = past_fn(i, mla_row, idx_row, diff_row)
        o_a = jnp.einsum('bthc,chd->bthd', _mla_attend(q_a, kv, q_pos, k_pos), w_uv_a[i])
        o_b = jnp.einsum('bthc,chd->bthd', _dsa_attend(q_b, q_idx, w_idx, k_idx, kv, q_pos, k_pos), w_uv_b[i])
        o_c = _diff_attend(q_d, dkv, q_pos, k_pos, diff_lambda[i], diff_subln[i], i)
        mix = jnp.concatenate([o_a.reshape(B, T, -1), o_b.reshape(B, T, -1), o_c.reshape(B, T, -1)], axis=-1)
        h = h + mix @ w_out[i]
        u = jax.nn.relu(_rms(h, norm_mlp[i]) @ w_up[i])
        h = h + (u * u) @ w_down[i]
        gate = jax.nn.sigmoid(_rms(h, norm_ple[i]) @ w_ple_gate[i])
        h = h + gate * (p[i] @ w_ple_proj[i])
    return _rms(h, norm_final), jnp.stack(mla_rows), jnp.stack(idx_rows), jnp.stack(diff_rows)


def _paged(cache_layer, page_table):
    rows = cache_layer[page_table]
    return rows.reshape(page_table.shape[0], -1, rows.shape[-1])


def setup_inputs(seed: int = 0) -> dict:
    key = jax.random.key(seed)
    ks = jax.random.split(key, 32)
    f32 = jnp.float32
    n_pages = PAST_LEN // PAGE_SIZE
    n_used = DEC_BATCH * n_pages
    n_pool = n_used + n_used // 4

    def nrm(k, shape, scale):
        return jax.random.normal(k, shape, f32) * scale

    def gain(k, shape):
        return 1.0 + 0.02 * jax.random.normal(k, shape, f32)

    page_table = jax.random.permutation(ks[5], n_pool)[:n_used].reshape(DEC_BATCH, n_pages).astype(jnp.int32)
    return {
        'x_prompt': nrm(ks[0], (BATCH, SEQ, D_MODEL), 1.0),
        'x_sample': nrm(ks[1], (DEC_BATCH, DEC_SEQ, D_MODEL), 1.0),
        'cache_mla': nrm(ks[2], (DEPTH, n_pool, PAGE_SIZE, MLA_ROW), 1.0),
        'cache_idx': nrm(ks[3], (DEPTH, n_pool, PAGE_SIZE, IDX_DIM), 1.0),
        'cache_diff': nrm(ks[4], (DEPTH, n_pool, PAGE_SIZE, DIFF_ROW), 1.0),
        'page_table': page_table,
        'p_prompt': nrm(ks[6], (DEPTH, BATCH, SEQ, PLE_DIM), 1.0),
        'p_sample': nrm(ks[7], (DEPTH, DEC_BATCH, DEC_SEQ, PLE_DIM), 1.0),
        'norm_attn': gain(ks[8], (DEPTH, D_MODEL)),
        'w_in': nrm(ks[9], (DEPTH, D_MODEL, IN_WIDTH), D_MODEL ** -0.5),
        'kv_norm': gain(ks[10], (DEPTH, KV_LORA)),
        'w_uk_a': nrm(ks[11], (DEPTH, KV_LORA, H_A, QK_NOPE), KV_LORA ** -0.5),
        'w_uv_a': nrm(ks[12], (DEPTH, KV_LORA, H_A, V_HEAD), KV_LORA ** -0.5),
        'w_uk_b': nrm(ks[13], (DEPTH, KV_LORA, H_B, QK_NOPE), KV_LORA ** -0.5),
        'w_uv_b': nrm(ks[14], (DEPTH, KV_LORA, H_B, V_HEAD), KV_LORA ** -0.5),
        'diff_lambda': nrm(ks[15], (DEPTH, 4, DIFF_D), 0.1),
        'diff_subln': gain(ks[16], (DEPTH, DIFF_V)),
        'w_out': nrm(ks[17], (DEPTH, MIX_WIDTH, D_MODEL), MIX_WIDTH ** -0.5),
        'norm_mlp': gain(ks[18], (DEPTH, D_MODEL)),
        'w_up': nrm(ks[19], (DEPTH, D_MODEL, D_FF), D_MODEL ** -0.5),
        'w_down': nrm(ks[20], (DEPTH, D_FF, D_MODEL), D_FF ** -0.5),
        'norm_ple': gain(ks[21], (DEPTH, D_MODEL)),
        'w_ple_gate': nrm(ks[22], (DEPTH, D_MODEL, D_MODEL), D_MODEL ** -0.5),
        'w_ple_proj': nrm(ks[23], (DEPTH, PLE_DIM, D_MODEL), PLE_DIM ** -0.5),
        'norm_final': gain(ks[24], (D_MODEL,)),
    }


def reference(x_prompt, x_sample, cache_mla, cache_idx, cache_diff, page_table, p_prompt, p_sample,
              norm_attn, w_in, kv_norm, w_uk_a, w_uv_a, w_uk_b, w_uv_b, diff_lambda, diff_subln,
              w_out, norm_mlp, w_up, w_down, norm_ple, w_ple_gate, w_ple_proj, norm_final):
    weights = (norm_attn, w_in, kv_norm, w_uk_a, w_uv_a, w_uk_b, w_uv_b, diff_lambda, diff_subln,
               w_out, norm_mlp, w_up, w_down, norm_ple, w_ple_gate, w_ple_proj, norm_final)

    prompt_pos = jnp.arange(x_prompt.shape[1], dtype=jnp.int32)

    def prompt_past(i, mla_row, idx_row, diff_row):
        return mla_row, idx_row, diff_row, prompt_pos

    y_prompt, mla_p, idx_p, diff_p = _layer_stack(x_prompt, p_prompt, prompt_pos, prompt_past, *weights)

    past_len = page_table.shape[1] * cache_mla.shape[2]
    n_new = x_sample.shape[1]
    sample_pos = past_len + jnp.arange(n_new, dtype=jnp.int32)
    key_pos = jnp.arange(past_len + n_new, dtype=jnp.int32)

    def sample_past(i, mla_row, idx_row, diff_row):
        kv = jnp.concatenate([_paged(cache_mla[i], page_table), mla_row], axis=1)
        k_idx = jnp.concatenate([_paged(cache_idx[i], page_table), idx_row], axis=1)
        dkv = jnp.concatenate([_paged(cache_diff[i], page_table), diff_row], axis=1)
        return kv, k_idx, dkv, key_pos

    y_sample, mla_s, idx_s, diff_s = _layer_stack(x_sample, p_sample, sample_pos, sample_past, *weights)
    return (y_prompt, y_sample, mla_p, idx_p, diff_p, mla_s, idx_s, diff_s)
```

```python
import functools
import math

import jax
import jax.numpy as jnp
import numpy as np
from jax import lax
from jax.experimental import pallas as pl
from jax.experimental.pallas import tpu as pltpu

F32 = jnp.float32
BF16 = jnp.bfloat16

H_A = 12
H_B = 8
H_AB = H_A + H_B
H_C = 24
QK_NOPE = 128
QK_ROPE = 64
V_HEAD = 128
KV_LORA = 256
IDX_HEADS = 32
IDX_DIM = 64
IDX_ROPE = 32
IDX_TOPK = 256
DIFF_D = 32
DIFF_V = 2 * DIFF_D
ROPE_THETA = 10000.0
EPS = 1e-6
MLA_ROW = KV_LORA + QK_ROPE
DIFF_ROW = 2 * DIFF_D + DIFF_V

LANES = 128
SUBLANES = 8
VMEM_LIMIT = 56 * 1024 * 1024

_IN_COLS = (H_A * QK_NOPE, H_A * QK_ROPE, KV_LORA, QK_ROPE, H_B * QK_NOPE, H_B * QK_ROPE,
            IDX_HEADS * IDX_DIM, IDX_DIM, IDX_HEADS, H_C * 2 * DIFF_D, 2 * DIFF_D, DIFF_V)
_IN_OFF = tuple(int(v) for v in np.concatenate([[0], np.cumsum(_IN_COLS)]))

O_QAN = 0
O_QAR = O_QAN + H_A * QK_NOPE
O_CKV = O_QAR + H_A * QK_ROPE
O_QBN = O_CKV + KV_LORA
O_QBR = O_QBN + H_B * QK_NOPE
O_QIDX = O_QBR + H_B * QK_ROPE
O_QD = O_QIDX + IDX_HEADS * IDX_DIM
O_KR = O_QD + H_C * 2 * DIFF_D
O_KI = O_KR + LANES
O_DROW = O_KI + LANES
O_W = O_DROW + LANES
IN_PAD = O_W + LANES

NEG_INF = float("-inf")
INT_MIN = np.int32(-2 ** 31)


def _cparams(*sem):
    return pltpu.CompilerParams(dimension_semantics=sem, vmem_limit_bytes=VMEM_LIMIT)


def _tile(n, pref):
    if n <= pref:
        return n
    t = pref
    while n % t:
        t //= 2
    return t


def _prep_w_in(w_in):
    d = w_in.shape[0]
    seg = [w_in[:, _IN_OFF[i]:_IN_OFF[i + 1]] for i in range(len(_IN_COLS))]
    (qa_n, qa_r, c_kv, k_r, qb_n, qb_r, q_idx, k_idx, w_idx, q_d, k_d, v_d) = seg
    z = lambda n: jnp.zeros((d, n), w_in.dtype)
    cols = [qa_n, qa_r, c_kv, qb_n, qb_r, q_idx, q_d,
            k_r, z(LANES - QK_ROPE), k_idx, z(LANES - IDX_DIM), k_d, v_d, w_idx, z(LANES - IDX_HEADS)]
    return jnp.concatenate(cols, axis=1).astype(BF16)


def _rope_tables(pos):
    t = pos.shape[0]

    def cs(half):
        inv = ROPE_THETA ** (-jnp.arange(half, dtype=F32) / half)
        ang = pos.astype(F32)[:, None] * inv[None, :]
        return jnp.cos(ang), jnp.sin(ang)

    c32, s32 = cs(QK_ROPE // 2)
    c16, s16 = cs(DIFF_D // 2)
    one = lambda n: jnp.ones((t, n), F32)
    zero = lambda n: jnp.zeros((t, n), F32)
    cat = lambda xs: jnp.concatenate(xs, axis=1)
    p64 = [cat([c32, c32] * 2), cat([-s32, zero(32)] * 2), cat([zero(32), s32] * 2)]
    pidx = [cat([c16, c16, one(32)] * 2), cat([-s16, zero(16), zero(32)] * 2),
            cat([zero(16), s16, zero(32)] * 2)]
    pd = [cat([c16, c16] * 4), cat([-s16, zero(16)] * 4), cat([zero(16), s16] * 4)]
    pr = [cat([c16, c16] * 2 + [one(64)]), cat([-s16, zero(16)] * 2 + [zero(64)]),
          cat([zero(16), s16] * 2 + [zero(64)])]
    return jnp.stack(p64 + pidx + pd + pr)


def _rms_kernel(x_ref, g_ref, o_ref):
    x = x_ref[...]
    y = x * lax.rsqrt(jnp.mean(x * x, axis=-1, keepdims=True) + EPS)
    o_ref[...] = (y * g_ref[...]).astype(o_ref.dtype)


def _rmsnorm(x, g, out_dtype):
    m, d = x.shape
    tm = _tile(m, 512)
    return pl.pallas_call(
        _rms_kernel,
        grid=(m // tm,),
        in_specs=[pl.BlockSpec((tm, d), lambda i: (i, 0)), pl.BlockSpec((1, d), lambda i: (0, 0))],
        out_specs=pl.BlockSpec((tm, d), lambda i: (i, 0)),
        out_shape=jax.ShapeDtypeStruct((m, d), out_dtype),
        compiler_params=_cparams("parallel"),
        name="rmsnorm",
    )(x, g.reshape(1, d))


def _mm_kernel(*refs, nk, act, has_res):
    x_ref, w_ref = refs[0], refs[1]
    res_ref = refs[2] if has_res else None
    o_ref = refs[2 + has_res]
    acc_ref = refs[3 + has_res] if nk > 1 else None
    part = jnp.dot(x_ref[...], w_ref[...], preferred_element_type=F32)

    def finish(acc):
        if act == "relu2":
            r = jnp.maximum(acc, 0.0)
            acc = r * r
        if has_res:
            acc = acc + res_ref[...]
        o_ref[...] = acc.astype(o_ref.dtype)

    if nk == 1:
        finish(part)
    else:
        k = pl.program_id(2)

        @pl.when(k == 0)
        def _():
            acc_ref[...] = part

        @pl.when(jnp.logical_and(k > 0, k < nk - 1))
        def _():
            acc_ref[...] += part

        @pl.when(k == nk - 1)
        def _():
            finish(acc_ref[...] + part)


def _matmul(x, w, *, res=None, act=None, out_dtype=F32, name="matmul"):
    m, kdim = x.shape
    n = w.shape[1]
    tm, tn, tk = _tile(m, 1024), _tile(n, 1024), (kdim if kdim <= 4096 else _tile(kdim, 2048))
    nk = kdim // tk
    has_res = res is not None
    in_specs = [pl.BlockSpec((tm, tk), lambda i, j, k: (i, k)), pl.BlockSpec((tk, tn), lambda i, j, k: (k, j))]
    args = [x, w]
    if has_res:
        in_specs.append(pl.BlockSpec((tm, tn), lambda i, j, k: (i, j)))
        args.append(res)
    return pl.pallas_call(
        functools.partial(_mm_kernel, nk=nk, act=act, has_res=has_res),
        grid=(m // tm, n // tn, nk),
        in_specs=in_specs,
        out_specs=pl.BlockSpec((tm, tn), lambda i, j, k: (i, j)),
        out_shape=jax.ShapeDtypeStruct((m, n), out_dtype),
        scratch_shapes=[pltpu.VMEM((tm, tn), F32)] if nk > 1 else [],
        compiler_params=_cparams("parallel", "parallel", "arbitrary"),
        name=name,
    )(*args)


def _ple_kernel(xn_ref, wg_ref, p_ref, wp_ref, h_ref, o_ref):
    z = jnp.dot(xn_ref[...], wg_ref[...], preferred_element_type=F32)
    gate = 1.0 / (1.0 + jnp.exp(-z))
    proj = jnp.dot(p_ref[...].astype(BF16), wp_ref[...], preferred_element_type=F32)
    o_ref[...] = h_ref[...] + gate * proj


def _ple(xn, w_gate, p, w_proj, h):
    m, d = xn.shape
    n = w_gate.shape[1]
    pd = p.shape[1]
    tm, tn = _tile(m, 512), _tile(n, 1024)
    return pl.pallas_call(
        _ple_kernel,
        grid=(m // tm, n // tn),
        in_specs=[pl.BlockSpec((tm, d), lambda i, j: (i, 0)), pl.BlockSpec((d, tn), lambda i, j: (0, j)),
                  pl.BlockSpec((tm, pd), lambda i, j: (i, 0)), pl.BlockSpec((pd, tn), lambda i, j: (0, j)),
                  pl.BlockSpec((tm, tn), lambda i, j: (i, j))],
        out_specs=pl.BlockSpec((tm, tn), lambda i, j: (i, j)),
        out_shape=jax.ShapeDtypeStruct((m, n), F32),
        compiler_params=_cparams("parallel", "parallel"),
        name="ple",
    )(xn, w_gate, p, w_proj, h)


def _rope128(x, tab_ref, pat, half):
    c, sa, sb = tab_ref[3 * pat], tab_ref[3 * pat + 1], tab_ref[3 * pat + 2]
    up = pltpu.roll(x, LANES - half, axis=1)
    dn = pltpu.roll(x, half, axis=1)
    return x * c + up * sa + dn * sb


def _post_kernel(proj_ref, tab_ref, kvn_ref, wuka_ref, wukb_ref,
                 qab_ref, qidx_ref, widx_ref, qd_ref,
                 mla_ref, idx_ref, drow_ref, mla16_ref, idx16_ref, drow16_ref):
    def col(off, width=LANES):
        return proj_ref[0, :, off:off + width]

    for h in range(H_A):
        qn = col(O_QAN + h * QK_NOPE).astype(BF16)
        qab_ref[0, h, :, 0:KV_LORA] = jnp.dot(qn, wuka_ref[h], preferred_element_type=F32).astype(BF16)
    for h in range(H_B):
        qn = col(O_QBN + h * QK_NOPE).astype(BF16)
        qab_ref[0, H_A + h, :, 0:KV_LORA] = jnp.dot(qn, wukb_ref[h], preferred_element_type=F32).astype(BF16)
    for base, off, nh in ((0, O_QAR, H_A), (H_A, O_QBR, H_B)):
        for c in range(nh // 2):
            y = _rope128(col(off + c * LANES), tab_ref, 0, QK_ROPE // 2).astype(BF16)
            qab_ref[0, base + 2 * c, :, KV_LORA:MLA_ROW] = y[:, 0:QK_ROPE]
            qab_ref[0, base + 2 * c + 1, :, KV_LORA:MLA_ROW] = y[:, QK_ROPE:LANES]
    for c in range(IDX_HEADS // 2):
        y = _rope128(col(O_QIDX + c * LANES), tab_ref, 1, IDX_ROPE // 2).astype(BF16)
        qidx_ref[0, 2 * c] = y[:, 0:IDX_DIM]
        qidx_ref[0, 2 * c + 1] = y[:, IDX_DIM:LANES]
    for c in range(H_C // 2):
        y = _rope128(col(O_QD + c * LANES), tab_ref, 2, DIFF_D // 2).astype(BF16)
        qd_ref[0, 2 * c] = y[:, 0:2 * DIFF_D]
        qd_ref[0, 2 * c + 1] = y[:, 2 * DIFF_D:LANES]
    wt = jnp.transpose(col(O_W) * (IDX_HEADS ** -0.5))
    widx_ref[0] = wt[0:IDX_HEADS, :]
    ckv = col(O_CKV, KV_LORA)
    lat = ckv * lax.rsqrt(jnp.mean(ckv * ckv, axis=-1, keepdims=True) + EPS) * kvn_ref[...]
    kr = _rope128(col(O_KR), tab_ref, 0, QK_ROPE // 2)[:, 0:QK_ROPE]
    mla_ref[0, :, 0:KV_LORA] = lat
    mla_ref[0, :, KV_LORA:MLA_ROW] = kr
    mla16_ref[0, :, 0:KV_LORA] = lat.astype(BF16)
    mla16_ref[0, :, KV_LORA:MLA_ROW] = kr.astype(BF16)
    ki = _rope128(col(O_KI), tab_ref, 1, IDX_ROPE // 2)[:, 0:IDX_DIM]
    idx_ref[0] = ki
    idx16_ref[0] = ki.astype(BF16)
    dr = _rope128(col(O_DROW), tab_ref, 3, DIFF_D // 2)
    drow_ref[0] = dr
    drow16_ref[0] = dr.astype(BF16)


def _post_project(proj, tables, kv_norm, wuk_a, wuk_b):
    b, t, _ = proj.shape
    tm = _tile(t, 256)
    bt = lambda w: pl.BlockSpec((1, tm, w), lambda i, j: (i, j, 0))
    hm = lambda h, w: pl.BlockSpec((1, h, tm, w), lambda i, j: (i, 0, j, 0))
    sds = jax.ShapeDtypeStruct
    return pl.pallas_call(
        _post_kernel,
        grid=(b, t // tm),
        in_specs=[bt(IN_PAD),
                  pl.BlockSpec((12, tm, LANES), lambda i, j: (0, j, 0)),
                  pl.BlockSpec((1, KV_LORA), lambda i, j: (0, 0)),
                  pl.BlockSpec((H_A, QK_NOPE, KV_LORA), lambda i, j: (0, 0, 0)),
                  pl.BlockSpec((H_B, QK_NOPE, KV_LORA), lambda i, j: (0, 0, 0))],
        out_specs=[hm(H_AB, MLA_ROW), hm(IDX_HEADS, IDX_DIM),
                   pl.BlockSpec((1, IDX_HEADS, tm), lambda i, j: (i, 0, j)),
                   hm(H_C, 2 * DIFF_D),
                   bt(MLA_ROW), bt(IDX_DIM), bt(DIFF_ROW), bt(MLA_ROW), bt(IDX_DIM), bt(DIFF_ROW)],
        out_shape=[sds((b, H_AB, t, MLA_ROW), BF16), sds((b, IDX_HEADS, t, IDX_DIM), BF16),
                   sds((b, IDX_HEADS, t), F32), sds((b, H_C, t, 2 * DIFF_D), BF16),
                   sds((b, t, MLA_ROW), F32), sds((b, t, IDX_DIM), F32), sds((b, t, DIFF_ROW), F32),
                   sds((b, t, MLA_ROW), BF16), sds((b, t, IDX_DIM), BF16), sds((b, t, DIFF_ROW), BF16)],
        compiler_params=_cparams("parallel", "parallel"),
        name="post_project",
    )(proj, tables, kv_norm.reshape(1, KV_LORA), wuk_a, wuk_b)


def _sort_key(score):
    bits = lax.bitcast_convert_type(score, jnp.int32)
    return jnp.where(bits < 0, bits ^ np.int32(0x7FFFFFFF), bits)


def _kth_largest_key(key, k, axis):
    shape = list(key.shape)
    shape[axis] = 1
    thr = jnp.full(shape, INT_MIN, jnp.int32)

    def count_ge(c):
        return jnp.sum(jnp.where(key >= c, 1.0, 0.0), axis=axis, keepdims=True)

    cand = jnp.zeros(shape, jnp.int32)
    thr = jnp.where(count_ge(cand) >= k, cand, thr)

    def body(i, thr):
        cand = thr | (jnp.int32(1) << (30 - i))
        return jnp.where(count_ge(cand) >= k, cand, thr)

    return lax.fori_loop(0, 31, body, thr)


def _softmax_rows(s):
    m = jnp.max(s, axis=-1, keepdims=True)
    p = jnp.exp(s - m)
    return p / jnp.sum(p, axis=-1, keepdims=True)


def _nt_dot(a, b):
    return lax.dot_general(a, b, (((1,), (1,)), ((), ())), preferred_element_type=F32)


def _prompt_latent_kernel(qab_ref, kv_ref, qidx_ref, widx_ref, kidx_ref, lat_ref, bias_ref, *, topk):
    tq = qab_ref.shape[2]
    t = kv_ref.shape[1]
    q0 = pl.program_id(1) * tq
    scale = (QK_NOPE + QK_ROPE) ** -0.5
    kv = kv_ref[0]

    kidx = kidx_ref[0]

    def idx_head(h, acc):
        rel = jnp.maximum(_nt_dot(kidx, qidx_ref[0, h]), 0.0)
        return acc + rel * widx_ref[0, pl.ds(h, 1), :]

    score = lax.fori_loop(0, IDX_HEADS, idx_head, jnp.zeros((t, tq), F32)) * (IDX_DIM ** -0.5)
    kpos_t = lax.broadcasted_iota(jnp.int32, (t, tq), 0)
    qpos_t = lax.broadcasted_iota(jnp.int32, (t, tq), 1) + q0
    causal_t = kpos_t <= qpos_t
    key = _sort_key(jnp.where(causal_t, score, NEG_INF))
    thr = _kth_largest_key(key, topk, axis=0)
    keep_t = jnp.logical_and(key >= thr, causal_t)
    bias_ref[...] = jnp.transpose(jnp.where(keep_t, 0.0, NEG_INF))

    kpos = lax.broadcasted_iota(jnp.int32, (tq, t), 1)
    qpos = lax.broadcasted_iota(jnp.int32, (tq, t), 0) + q0
    causal = kpos <= qpos

    def attend(h, masked_scores):
        p = _softmax_rows(masked_scores).astype(BF16)
        lat_ref[0, h] = jnp.dot(p, kv[:, 0:KV_LORA], preferred_element_type=F32).astype(lat_ref.dtype)

    def mla_head(h, carry):
        s = _nt_dot(qab_ref[0, h], kv) * scale
        attend(h, jnp.where(causal, s, NEG_INF))
        return carry

    def dsa_head(h, carry):
        s = _nt_dot(qab_ref[0, h], kv) * scale
        attend(h, s + bias_ref[...])
        return carry

    lax.fori_loop(0, H_A, mla_head, 0)
    lax.fori_loop(H_A, H_AB, dsa_head, 0)


def _prompt_latent_attention(qab, kv16, qidx, widx, kidx16):
    b, _, t, _ = qab.shape
    tq = _tile(t, 128)
    topk = min(IDX_TOPK, t // 4)
    return pl.pallas_call(
        functools.partial(_prompt_latent_kernel, topk=topk),
        grid=(b, t // tq),
        in_specs=[pl.BlockSpec((1, H_AB, tq, MLA_ROW), lambda i, j: (i, 0, j, 0)),
                  pl.BlockSpec((1, t, MLA_ROW), lambda i, j: (i, 0, 0)),
                  pl.BlockSpec((1, IDX_HEADS, tq, IDX_DIM), lambda i, j: (i, 0, j, 0)),
                  pl.BlockSpec((1, IDX_HEADS, tq), lambda i, j: (i, 0, j)),
                  pl.BlockSpec((1, t, IDX_DIM), lambda i, j: (i, 0, 0))],
        out_specs=pl.BlockSpec((1, H_AB, tq, KV_LORA), lambda i, j: (i, 0, j, 0)),
        out_shape=jax.ShapeDtypeStruct((b, H_AB, t, KV_LORA), BF16),
        scratch_shapes=[pltpu.VMEM((tq, t), F32)],
        compiler_params=_cparams("parallel", "parallel"),
        name="prompt_latent_attention",
    )(qab, kv16, qidx, widx, kidx16)


def _diff_lambda(lam_ref, lam_init):
    lv = lam_ref[...]
    a = jnp.exp(jnp.sum(lv[0:1] * lv[1:2], axis=-1, keepdims=True))
    b = jnp.exp(jnp.sum(lv[2:3] * lv[3:4], axis=-1, keepdims=True))
    return a - b + lam_init


def _split_q(q):
    lane = lax.broadcasted_iota(jnp.int32, q.shape, 1)
    zero = jnp.zeros_like(q)
    return jnp.concatenate([jnp.where(lane < DIFF_D, q, zero), jnp.where(lane >= DIFF_D, q, zero)], axis=0)


def _subln(o, g_ref, lam_init):
    y = o * lax.rsqrt(jnp.mean(o * o, axis=-1, keepdims=True) + EPS)
    return (y * g_ref[...]) * (1.0 - lam_init)


def _prompt_diff_kernel(qd_ref, dkv_ref, lam_ref, g_ref, o_ref, *, lam_init):
    tq = qd_ref.shape[2]
    t = dkv_ref.shape[1]
    q0 = pl.program_id(1) * tq
    scale = DIFF_D ** -0.5
    lam = _diff_lambda(lam_ref, lam_init)
    rows = dkv_ref[0]
    k = rows[:, 0:2 * DIFF_D]
    kpos = lax.broadcasted_iota(jnp.int32, (2 * tq, t), 1)
    r = lax.broadcasted_iota(jnp.int32, (2 * tq, t), 0)
    qpos = jnp.where(r >= tq, r - tq, r) + q0
    causal = kpos <= qpos

    def head(h, carry):
        s = _nt_dot(_split_q(qd_ref[0, h]), k) * scale
        p = _softmax_rows(jnp.where(causal, s, NEG_INF))
        a = (p[0:tq] - lam * p[tq:2 * tq]).astype(BF16)
        o = jnp.dot(a, rows, preferred_element_type=F32)[:, 2 * DIFF_D:DIFF_ROW]
        o_ref[0, h] = _subln(o, g_ref, lam_init).astype(o_ref.dtype)
        return carry

    lax.fori_loop(0, H_C, head, 0)


def _prompt_diff_attention(qd, dkv16, lam_vec, subln, lam_init):
    b, _, t, _ = qd.shape
    tq = _tile(t, 128)
    return pl.pallas_call(
        functools.partial(_prompt_diff_kernel, lam_init=lam_init),
        grid=(b, t // tq),
        in_specs=[pl.BlockSpec((1, H_C, tq, 2 * DIFF_D), lambda i, j: (i, 0, j, 0)),
                  pl.BlockSpec((1, t, DIFF_ROW), lambda i, j: (i, 0, 0)),
                  pl.BlockSpec((4, DIFF_D), lambda i, j: (0, 0)),
                  pl.BlockSpec((1, DIFF_V), lambda i, j: (0, 0))],
        out_specs=pl.BlockSpec((1, H_C, tq, DIFF_V), lambda i, j: (i, 0, j, 0)),
        out_shape=jax.ShapeDtypeStruct((b, H_C, t, DIFF_V), BF16),
        compiler_params=_cparams("parallel", "parallel"),
        name="prompt_diff_attention",
    )(qd, dkv16, lam_vec, subln.reshape(1, DIFF_V))


def _uv_kernel(lat_ref, w_ref, o_ref):
    o_ref[...] = jnp.dot(lat_ref[0, 0], w_ref[0], preferred_element_type=F32).astype(o_ref.dtype)


def _uv_project(lat, wuv):
    b, h, t, c = lat.shape
    tt = _tile(t, 512)
    nt = t // tt
    return pl.pallas_call(
        _uv_kernel,
        grid=(b, nt, h),
        in_specs=[pl.BlockSpec((1, 1, tt, c), lambda i, j, k: (i, k, j, 0)),
                  pl.BlockSpec((1, c, V_HEAD), lambda i, j, k: (k, 0, 0))],
        out_specs=pl.BlockSpec((tt, V_HEAD), lambda i, j, k: (i * nt + j, k)),
        out_shape=jax.ShapeDtypeStruct((b * t, h * V_HEAD), BF16),
        compiler_params=_cparams("parallel", "parallel", "parallel"),
        name="uv_project",
    )(lat, wuv)


def _page_copies(cache_ref, pt_ref, buf, sem, layer, seq, slot, n_pages):
    def copy(j):
        return pltpu.make_async_copy(cache_ref.at[layer, pt_ref[seq, j]], buf.at[slot, j], sem.at[slot])
    return copy


def _start_pages(cache_ref, pt_ref, buf, sem, layer, seq, slot, n_pages):
    copy = _page_copies(cache_ref, pt_ref, buf, sem, layer, seq, slot, n_pages)

    def body(j, c):
        copy(j).start()
        return c

    lax.fori_loop(0, n_pages, body, 0)


def _wait_pages(cache_ref, pt_ref, buf, sem, layer, seq, slot, n_pages):
    copy = _page_copies(cache_ref, pt_ref, buf, sem, layer, seq, slot, n_pages)

    def body(j, c):
        copy(j).wait()
        return c

    lax.fori_loop(0, n_pages, body, 0)


def _paged_pipeline(cache_ref, pt_ref, buf, sem, layer, n_pages):
    b = pl.program_id(0)
    nb = pl.num_programs(0)
    slot = b % 2

    @pl.when(b == 0)
    def _():
        _start_pages(cache_ref, pt_ref, buf, sem, layer, 0, 0, n_pages)

    @pl.when(b + 1 < nb)
    def _():
        _start_pages(cache_ref, pt_ref, buf, sem, layer, b + 1, 1 - slot, n_pages)

    _wait_pages(cache_ref, pt_ref, buf, sem, layer, b, slot, n_pages)
    return slot


def _sample_index_kernel(pt_ref, cache_ref, q_ref, w_ref, knew_ref, o_ref, buf, sem, *, layer, chunk_pages):
    n_pages, page = buf.shape[1], buf.shape[2]
    slot = _paged_pipeline(cache_ref, pt_ref, buf, sem, layer, n_pages)
    q = q_ref[0]
    w = w_ref[0]
    ck = chunk_pages * page

    def chunk(c, carry):
        kc = buf[slot, pl.ds(c * chunk_pages, chunk_pages)].reshape(ck, IDX_DIM).astype(BF16)
        rel = jnp.maximum(_nt_dot(q, kc), 0.0)
        sc = jnp.sum(rel * w, axis=0, keepdims=True) * (IDX_DIM ** -0.5)
        o_ref[0, :, pl.ds(pl.multiple_of(c * ck, ck), ck)] = sc
        return carry

    lax.fori_loop(0, n_pages // chunk_pages, chunk, 0)
    knew = knew_ref[0].astype(BF16).astype(F32)
    rel = jnp.maximum(jnp.sum(q.astype(F32) * knew, axis=-1, keepdims=True), 0.0)
    s_new = jnp.sum(rel * w, axis=0, keepdims=True) * (IDX_DIM ** -0.5)
    lane = lax.broadcasted_iota(jnp.int32, (1, LANES), 1)
    o_ref[0, :, n_pages * page:n_pages * page + LANES] = jnp.where(lane == 0, s_new, NEG_INF)


def _sample_index_scores(page_table, cache_idx, layer, q_idx, w_idx, k_new):
    nb, n_pages = page_table.shape
    page = cache_idx.shape[2]
    past = n_pages * page
    chunk_pages = _tile(n_pages, 8)
    grid_spec = pltpu.PrefetchScalarGridSpec(
        num_scalar_prefetch=1,
        grid=(nb,),
        in_specs=[pl.BlockSpec(memory_space=pl.ANY),
                  pl.BlockSpec((1, IDX_HEADS, IDX_DIM), lambda b, pt: (b, 0, 0)),
                  pl.BlockSpec((1, IDX_HEADS, 1), lambda b, pt: (b, 0, 0)),
                  pl.BlockSpec((1, 1, IDX_DIM), lambda b, pt: (b, 0, 0))],
        out_specs=pl.BlockSpec((1, 1, past + LANES), lambda b, pt: (b, 0, 0)),
        scratch_shapes=[pltpu.VMEM((2, n_pages, page, IDX_DIM), F32), pltpu.SemaphoreType.DMA((2,))],
    )
    return pl.pallas_call(
        functools.partial(_sample_index_kernel, layer=layer, chunk_pages=chunk_pages),
        grid_spec=grid_spec,
        out_shape=jax.ShapeDtypeStruct((nb, 1, past + LANES), F32),
        compiler_params=_cparams("arbitrary"),
        name="sample_index_scores",
    )(page_table, cache_idx, q_idx, w_idx, k_new)


def _topk_bias_kernel(s_ref, o_ref, *, topk):
    key = _sort_key(s_ref[...])
    thr = _kth_largest_key(key, topk, axis=1)
    o_ref[...] = jnp.where(key >= thr, 0.0, NEG_INF)


def _topk_bias(scores, topk):
    r, n = scores.shape
    tr = _tile(r, 128)
    return pl.pallas_call(
        functools.partial(_topk_bias_kernel, topk=topk),
        grid=(r // tr,),
        in_specs=[pl.BlockSpec((tr, n), lambda i: (i, 0))],
        out_specs=pl.BlockSpec((tr, n), lambda i: (i, 0)),
        out_shape=jax.ShapeDtypeStruct((r, n), F32),
        compiler_params=_cparams("parallel"),
        name="topk_bias",
    )(scores)


def _sample_latent_kernel(pt_ref, cache_ref, q_ref, bias_ref, new_ref, o_ref, buf, sem, *, layer, chunk_pages):
    n_pages, page = buf.shape[1], buf.shape[2]
    hp = q_ref.shape[1]
    slot = _paged_pipeline(cache_ref, pt_ref, buf, sem, layer, n_pages)
    scale = (QK_NOPE + QK_ROPE) ** -0.5
    q = q_ref[0]
    ck = chunk_pages * page
    row = lax.broadcasted_iota(jnp.int32, (hp, 1), 0)
    is_dsa = jnp.logical_and(row >= H_A, row < H_AB)

    def masked(s, bias):
        return jnp.where(is_dsa, s + bias, s)

    def chunk(c, carry):
        m, l, acc = carry
        kc = buf[slot, pl.ds(c * chunk_pages, chunk_pages)].reshape(ck, MLA_ROW).astype(BF16)
        bias = bias_ref[0, :, pl.ds(pl.multiple_of(c * ck, ck), ck)]
        s = masked(_nt_dot(q, kc) * scale, bias)
        m_new = jnp.maximum(m, jnp.max(s, axis=-1, keepdims=True))
        alpha = jnp.exp(m - m_new)
        p = jnp.exp(s - m_new)
        l = alpha * l + jnp.sum(p, axis=-1, keepdims=True)
        acc = alpha * acc + jnp.dot(p.astype(BF16), kc[:, 0:KV_LORA], preferred_element_type=F32)
        return m_new, l, acc

    new = new_ref[0].astype(BF16).astype(F32)
    lane = lax.broadcasted_iota(jnp.int32, (1, LANES), 1)
    bias_new = jnp.sum(jnp.where(lane == 0, bias_ref[0, :, n_pages * page:n_pages * page + LANES], 0.0),
                       axis=-1, keepdims=True)
    s_new = masked(jnp.sum(q.astype(F32) * new, axis=-1, keepdims=True) * scale, bias_new)
    m0 = s_new
    l0 = jnp.where(s_new > NEG_INF, 1.0, 0.0)
    acc0 = l0 * new[:, 0:KV_LORA]

    def chunk_guarded(c, carry):
        m, l, acc = carry
        m_safe = jnp.where(m > NEG_INF, m, jnp.float32(-1e30))
        return chunk(c, (m_safe, l, acc))

    m, l, acc = lax.fori_loop(0, n_pages // chunk_pages, chunk_guarded, (m0, l0, acc0))
    o_ref[0] = (acc / l).astype(o_ref.dtype)


def _sample_latent_attention(page_table, cache_mla, layer, q, bias, new_rows):
    nb, n_pages = page_table.shape
    page = cache_mla.shape[2]
    past = n_pages * page
    hp = q.shape[1]
    chunk_pages = _tile(n_pages, 8)
    grid_spec = pltpu.PrefetchScalarGridSpec(
        num_scalar_prefetch=1,
        grid=(nb,),
        in_specs=[pl.BlockSpec(memory_space=pl.ANY),
                  pl.BlockSpec((1, hp, MLA_ROW), lambda b, pt: (b, 0, 0)),
                  pl.BlockSpec((1, 1, past + LANES), lambda b, pt: (b, 0, 0)),
                  pl.BlockSpec((1, 1, MLA_ROW), lambda b, pt: (b, 0, 0))],
        out_specs=pl.BlockSpec((1, hp, KV_LORA), lambda b, pt: (b, 0, 0)),
        scratch_shapes=[pltpu.VMEM((2, n_pages, page, MLA_ROW), F32), pltpu.SemaphoreType.DMA((2,))],
    )
    return pl.pallas_call(
        functools.partial(_sample_latent_kernel, layer=layer, chunk_pages=chunk_pages),
        grid_spec=grid_spec,
        out_shape=jax.ShapeDtypeStruct((nb, hp, KV_LORA), BF16),
        compiler_params=_cparams("arbitrary"),
        name="sample_latent_attention",
    )(page_table, cache_mla, q, bias, new_rows)


def _sample_diff_kernel(pt_ref, cache_ref, q_ref, new_ref, lam_ref, g_ref, o_ref, buf, sem, *,
                        layer, chunk_pages, lam_init):
    n_pages, page = buf.shape[1], buf.shape[2]
    slot = _paged_pipeline(cache_ref, pt_ref, buf, sem, layer, n_pages)
    scale = DIFF_D ** -0.5
    lam = _diff_lambda(lam_ref, lam_init)
    qs = _split_q(q_ref[0])
    ck = chunk_pages * page

    new = new_ref[0].astype(BF16).astype(F32)
    m0 = jnp.sum(qs.astype(F32) * new[:, 0:2 * DIFF_D], axis=-1, keepdims=True) * scale
    l0 = jnp.ones_like(m0)
    acc0 = jnp.broadcast_to(new, (2 * H_C, DIFF_ROW))

    def chunk(c, carry):
        m, l, acc = carry
        rc = buf[slot, pl.ds(c * chunk_pages, chunk_pages)].reshape(ck, DIFF_ROW).astype(BF16)
        s = _nt_dot(qs, rc[:, 0:2 * DIFF_D]) * scale
        m_new = jnp.maximum(m, jnp.max(s, axis=-1, keepdims=True))
        alpha = jnp.exp(m - m_new)
        p = jnp.exp(s - m_new)
        l = alpha * l + jnp.sum(p, axis=-1, keepdims=True)
        acc = alpha * acc + jnp.dot(p.astype(BF16), rc, preferred_element_type=F32)
        return m_new, l, acc

    m, l, acc = lax.fori_loop(0, n_pages // chunk_pages, chunk, (m0, l0, acc0))
    o2 = acc / l
    o = (o2[0:H_C] - lam * o2[H_C:2 * H_C])[:, 2 * DIFF_D:DIFF_ROW]
    o_ref[0] = _subln(o, g_ref, lam_init).astype(o_ref.dtype)


def _sample_diff_attention(page_table, cache_diff, layer, q, new_rows, lam_vec, subln, lam_init):
    nb, n_pages = page_table.shape
    page = cache_diff.shape[2]
    chunk_pages = _tile(n_pages, 8)
    grid_spec = pltpu.PrefetchScalarGridSpec(
        num_scalar_prefetch=1,
        grid=(nb,),
        in_specs=[pl.BlockSpec(memory_space=pl.ANY),
                  pl.BlockSpec((1, H_C, 2 * DIFF_D), lambda b, pt: (b, 0, 0)),
                  pl.BlockSpec((1, 1, DIFF_ROW), lambda b, pt: (b, 0, 0)),
                  pl.BlockSpec((4, DIFF_D), lambda b, pt: (0, 0)),
                  pl.BlockSpec((1, DIFF_V), lambda b, pt: (0, 0))],
        out_specs=pl.BlockSpec((1, H_C, DIFF_V), lambda b, pt: (b, 0, 0)),
        scratch_shapes=[pltpu.VMEM((2, n_pages, page, DIFF_ROW), F32), pltpu.SemaphoreType.DMA((2,))],
    )
    return pl.pallas_call(
        functools.partial(_sample_diff_kernel, layer=layer, chunk_pages=chunk_pages, lam_init=lam_init),
        grid_spec=grid_spec,
        out_shape=jax.ShapeDtypeStruct((nb, H_C, DIFF_V), BF16),
        compiler_params=_cparams("arbitrary"),
        name="sample_diff_attention",
    )(page_table, cache_diff, q, new_rows, lam_vec, subln.reshape(1, DIFF_V))


def _lam_init(layer):
    return 0.8 - 0.6 * math.exp(-0.3 * layer)


def _prep_weights(w_in, w_uk_a, w_uv_a, w_uk_b, w_uv_b, w_out, w_up, w_down, w_ple_gate, w_ple_proj):
    depth = w_in.shape[0]
    per_layer = []
    for i in range(depth):
        wuk_a = jnp.transpose(w_uk_a[i], (1, 2, 0)).astype(BF16)
        wuk_b = jnp.transpose(w_uk_b[i], (1, 2, 0)).astype(BF16)
        wuv = jnp.concatenate([jnp.transpose(w_uv_a[i], (1, 0, 2)), jnp.transpose(w_uv_b[i], (1, 0, 2))],
                              axis=0).astype(BF16)
        per_layer.append(dict(
            w_in=_prep_w_in(w_in[i]), wuk_a=wuk_a, wuk_b=wuk_b, wuv=wuv,
            w_out=w_out[i].astype(BF16), w_up=w_up[i].astype(BF16), w_down=w_down[i].astype(BF16),
            w_gate=w_ple_gate[i].astype(BF16), w_proj=w_ple_proj[i].astype(BF16)))
    return per_layer


def _layer_tail(h, mix, p, lw, norm_mlp, norm_ple):
    h = _matmul(mix, lw["w_out"], res=h, name="out_proj")
    u = _matmul(_rmsnorm(h, norm_mlp, BF16), lw["w_up"], act="relu2", out_dtype=BF16, name="mlp_up")
    h = _matmul(u, lw["w_down"], res=h, name="mlp_down")
    return _ple(_rmsnorm(h, norm_ple, BF16), lw["w_gate"], p, lw["w_proj"], h)


def _prompt_stack(x, p, layers, norm_attn, kv_norm, diff_lambda, diff_subln, norm_mlp, norm_ple, norm_final):
    b, t, d = x.shape
    m = b * t
    tables = _rope_tables(jnp.arange(t, dtype=jnp.int32))
    h = x.reshape(m, d)
    mla_rows, idx_rows, diff_rows = [], [], []
    for i, lw in enumerate(layers):
        proj = _matmul(_rmsnorm(h, norm_attn[i], BF16), lw["w_in"], name="in_proj").reshape(b, t, IN_PAD)
        (qab, qidx, widx, qd, mla, idx, drow, mla16, idx16, drow16) = _post_project(
            proj, tables, kv_norm[i], lw["wuk_a"], lw["wuk_b"])
        mla_rows.append(mla)
        idx_rows.append(idx)
        diff_rows.append(drow)
        lat = _prompt_latent_attention(qab, mla16, qidx, widx, idx16)
        o_ab = _uv_project(lat, lw["wuv"])
        o_c = _prompt_diff_attention(qd, drow16, diff_lambda[i], diff_subln[i], _lam_init(i))
        o_c = jnp.transpose(o_c, (0, 2, 1, 3)).reshape(m, H_C * DIFF_V)
        mix = jnp.concatenate([o_ab, o_c], axis=1)
        h = _layer_tail(h, mix, p[i].reshape(m, -1), lw, norm_mlp[i], norm_ple[i])
    y = _rmsnorm(h, norm_final, F32).reshape(b, t, d)
    return y, jnp.stack(mla_rows), jnp.stack(idx_rows), jnp.stack(diff_rows)


def _sample_stack(x, p, page_table, cache_mla, cache_idx, cache_diff, layers, norm_attn, kv_norm,
                  diff_lambda, diff_subln, norm_mlp, norm_ple, norm_final):
    nb, n_new, d = x.shape
    assert n_new == 1, "the sample kernels take one new token per sequence"
    past = page_table.shape[1] * cache_mla.shape[2]
    topk = min(IDX_TOPK, (past + n_new) // 4)
    tables = _rope_tables(jnp.full((nb,), past, jnp.int32))
    h = x.reshape(nb, d)
    hp = -(-H_AB // 16) * 16
    mla_rows, idx_rows, diff_rows = [], [], []
    for i, lw in enumerate(layers):
        proj = _matmul(_rmsnorm(h, norm_attn[i], BF16), lw["w_in"], name="in_proj").reshape(1, nb, IN_PAD)
        (qab, qidx, widx, qd, mla, idx, drow, _, _, _) = _post_project(
            proj, tables, kv_norm[i], lw["wuk_a"], lw["wuk_b"])
        mla_rows.append(mla.reshape(nb, 1, MLA_ROW))
        idx_rows.append(idx.reshape(nb, 1, IDX_DIM))
        diff_rows.append(drow.reshape(nb, 1, DIFF_ROW))
        q_tok = jnp.transpose(qab[0], (1, 0, 2))
        q_tok = jnp.pad(q_tok, ((0, 0), (0, hp - H_AB), (0, 0)))
        qidx_tok = jnp.transpose(qidx[0], (1, 0, 2))
        widx_tok = jnp.transpose(widx[0], (1, 0)).reshape(nb, IDX_HEADS, 1)
        qd_tok = jnp.transpose(qd[0], (1, 0, 2))
        scores = _sample_index_scores(page_table, cache_idx, i, qidx_tok, widx_tok, idx_rows[-1])
        bias = _topk_bias(scores.reshape(nb, past + LANES), topk).reshape(nb, 1, past + LANES)
        lat = _sample_latent_attention(page_table, cache_mla, i, q_tok, bias, mla_rows[-1])
        lat = jnp.transpose(lat[:, :H_AB], (1, 0, 2))[None]
        o_ab = _uv_project(lat, lw["wuv"])
        o_c = _sample_diff_attention(page_table, cache_diff, i, qd_tok, diff_rows[-1], diff_lambda[i],
                                     diff_subln[i], _lam_init(i)).reshape(nb, H_C * DIFF_V)
        mix = jnp.concatenate([o_ab, o_c], axis=1)
        h = _layer_tail(h, mix, p[i].reshape(nb, -1), lw, norm_mlp[i], norm_ple[i])
    y = _rmsnorm(h, norm_final, F32).reshape(nb, n_new, d)
    return y, jnp.stack(mla_rows), jnp.stack(idx_rows), jnp.stack(diff_rows)


def kernel(x_prompt, x_sample, cache_mla, cache_idx, cache_diff, page_table, p_prompt, p_sample, norm_attn, w_in, kv_norm, w_uk_a, w_uv_a, w_uk_b, w_uv_b, diff_lambda, diff_subln, w_out, norm_mlp, w_up, w_down, norm_ple, w_ple_gate, w_ple_proj, norm_final):
    layers = _prep_weights(w_in, w_uk_a, w_uv_a, w_uk_b, w_uv_b, w_out, w_up, w_down, w_ple_gate, w_ple_proj)
    y_p, mla_p, idx_p, diff_p = _prompt_stack(
        x_prompt, p_prompt, layers, norm_attn, kv_norm, diff_lambda, diff_subln, norm_mlp, norm_ple, norm_final)
    y_s, mla_s, idx_s, diff_s = _sample_stack(
        x_sample, p_sample, page_table, cache_mla, cache_idx, cache_diff, layers, norm_attn, kv_norm,
        diff_lambda, diff_subln, norm_mlp, norm_ple, norm_final)
    return (y_p, y_s, mla_p, idx_p, diff_p, mla_s, idx_s, diff_s)
```

```python
import functools
import math

import jax
import jax.numpy as jnp
import numpy as np
from jax import lax
from jax.experimental import pallas as pl
from jax.experimental.pallas import tpu as pltpu

F32 = jnp.float32
BF16 = jnp.bfloat16

H_A = 12
H_B = 8
H_AB = H_A + H_B
H_C = 24
QK_NOPE = 128
QK_ROPE = 64
V_HEAD = 128
KV_LORA = 256
IDX_HEADS = 32
IDX_DIM = 64
IDX_ROPE = 32
IDX_TOPK = 256
DIFF_D = 32
DIFF_V = 2 * DIFF_D
ROPE_THETA = 10000.0
EPS = 1e-6
MLA_ROW = KV_LORA + QK_ROPE
DIFF_ROW = 2 * DIFF_D + DIFF_V
MIX_AB = H_AB * V_HEAD
MIX_C = H_C * DIFF_V

LANES = 128
SUBLANES = 8
VMEM_LIMIT = 56 * 1024 * 1024
KEY_CHUNK = 2 * LANES
LOG2E = math.log2(math.e)

_IN_COLS = (H_A * QK_NOPE, H_A * QK_ROPE, KV_LORA, QK_ROPE, H_B * QK_NOPE, H_B * QK_ROPE,
            IDX_HEADS * IDX_DIM, IDX_DIM, IDX_HEADS, H_C * 2 * DIFF_D, 2 * DIFF_D, DIFF_V)
_IN_OFF = tuple(int(v) for v in np.concatenate([[0], np.cumsum(_IN_COLS)]))

O_QAN = 0
O_QAR = O_QAN + H_A * QK_NOPE
O_CKV = O_QAR + H_A * QK_ROPE
O_QBN = O_CKV + KV_LORA
O_QBR = O_QBN + H_B * QK_NOPE
O_QIDX = O_QBR + H_B * QK_ROPE
O_QD = O_QIDX + IDX_HEADS * IDX_DIM
O_KR = O_QD + H_C * 2 * DIFF_D
O_KI = O_KR + LANES
O_DROW = O_KI + LANES
O_W = O_DROW + LANES
IN_PAD = O_W + LANES

NEG_INF = float("-inf")
INT_MIN = np.int32(-2 ** 31)


def _cparams(*sem):
    return pltpu.CompilerParams(dimension_semantics=sem, vmem_limit_bytes=VMEM_LIMIT)


def _tile(n, pref):
    if n <= pref:
        return n
    t = pref
    while n % t:
        t //= 2
    return t


def _prep_w_in(w_in):
    seg = [w_in[..., _IN_OFF[i]:_IN_OFF[i + 1]] for i in range(len(_IN_COLS))]
    (qa_n, qa_r, c_kv, k_r, qb_n, qb_r, q_idx, k_idx, w_idx, q_d, k_d, v_d) = seg
    z = lambda n: jnp.zeros(w_in.shape[:-1] + (n,), w_in.dtype)
    cols = [qa_n, qa_r, c_kv, qb_n, qb_r, q_idx, q_d,
            k_r, z(LANES - QK_ROPE), k_idx, z(LANES - IDX_DIM), k_d, v_d, w_idx, z(LANES - IDX_HEADS)]
    return jnp.concatenate(cols, axis=-1).astype(BF16)


def _prep_weights(w_in, w_uk_a, w_uv_a, w_uk_b, w_uv_b, w_out, w_up, w_down, w_ple_gate, w_ple_proj):
    wuv = jnp.concatenate([jnp.transpose(w_uv_a, (0, 2, 1, 3)), jnp.transpose(w_uv_b, (0, 2, 1, 3))], axis=1)
    return dict(
        w_in=_prep_w_in(w_in),
        wuk_a=jnp.transpose(w_uk_a, (0, 2, 3, 1)).astype(BF16),
        wuk_b=jnp.transpose(w_uk_b, (0, 2, 3, 1)).astype(BF16),
        wuv=wuv.astype(BF16),
        w_out_ab=w_out[:, :MIX_AB].astype(BF16), w_out_c=w_out[:, MIX_AB:].astype(BF16),
        w_up=w_up.astype(BF16), w_down=w_down.astype(BF16),
        w_gate=w_ple_gate.astype(BF16), w_proj=w_ple_proj.astype(BF16))


def _rope_tables(pos):
    t = pos.shape[0]

    def cs(half):
        inv = ROPE_THETA ** (-jnp.arange(half, dtype=F32) / half)
        ang = pos.astype(F32)[:, None] * inv[None, :]
        return jnp.cos(ang), jnp.sin(ang)

    c32, s32 = cs(QK_ROPE // 2)
    c16, s16 = cs(DIFF_D // 2)
    one = lambda n: jnp.ones((t, n), F32)
    zero = lambda n: jnp.zeros((t, n), F32)
    cat = lambda xs: jnp.concatenate(xs, axis=1)
    p64 = [cat([c32, c32] * 2), cat([-s32, zero(32)] * 2), cat([zero(32), s32] * 2)]
    pidx = [cat([c16, c16, one(32)] * 2), cat([-s16, zero(16), zero(32)] * 2),
            cat([zero(16), s16, zero(32)] * 2)]
    pd = [cat([c16, c16] * 4), cat([-s16, zero(16)] * 4), cat([zero(16), s16] * 4)]
    pr = [cat([c16, c16] * 2 + [one(64)]), cat([-s16, zero(16)] * 2 + [zero(64)]),
          cat([zero(16), s16] * 2 + [zero(64)])]
    return jnp.stack(p64 + pidx + pd + pr)


def _rms_kernel(x_ref, g_ref, o_ref):
    x = x_ref[...]
    y = x * lax.rsqrt(jnp.mean(x * x, axis=-1, keepdims=True) + EPS)
    o_ref[...] = (y * g_ref[...]).astype(o_ref.dtype)


def _rmsnorm(x, g, out_dtype):
    m, d = x.shape
    tm = _tile(m, 512)
    return pl.pallas_call(
        _rms_kernel,
        grid=(m // tm,),
        in_specs=[pl.BlockSpec((tm, d), lambda i: (i, 0)), pl.BlockSpec((1, d), lambda i: (0, 0))],
        out_specs=pl.BlockSpec((tm, d), lambda i: (i, 0)),
        out_shape=jax.ShapeDtypeStruct((m, d), out_dtype),
        compiler_params=_cparams("parallel"),
        name="rmsnorm",
    )(x, g.reshape(1, d))


def _mm_kernel(*refs, nk, act, has_res):
    x_ref, w_ref = refs[0], refs[1]
    res_ref = refs[2] if has_res else None
    o_ref = refs[2 + has_res]
    acc_ref = refs[3 + has_res] if nk > 1 else None
    part = jnp.dot(x_ref[...], w_ref[...], preferred_element_type=F32)

    def finish(acc):
        if act == "relu2":
            r = jnp.maximum(acc, 0.0)
            acc = r * r
        if has_res:
            acc = acc + res_ref[...]
        o_ref[...] = acc.astype(o_ref.dtype)

    if nk == 1:
        finish(part)
    else:
        k = pl.program_id(2)

        @pl.when(k == 0)
        def _():
            acc_ref[...] = part

        @pl.when(jnp.logical_and(k > 0, k < nk - 1))
        def _():
            acc_ref[...] += part

        @pl.when(k == nk - 1)
        def _():
            finish(acc_ref[...] + part)


def _matmul(x, w, layer, *, res=None, act=None, out_dtype=F32, name="matmul"):
    m, kdim = x.shape
    n = w.shape[2]
    tm, tn, tk = _tile(m, 1024), _tile(n, 1024), (kdim if kdim <= 4096 else _tile(kdim, 2048))
    nk = kdim // tk
    has_res = res is not None
    in_specs = [pl.BlockSpec((tm, tk), lambda i, j, k: (i, k)),
                pl.BlockSpec((None, tk, tn), lambda i, j, k: (layer, k, j))]
    args = [x, w]
    if has_res:
        in_specs.append(pl.BlockSpec((tm, tn), lambda i, j, k: (i, j)))
        args.append(res)
    return pl.pallas_call(
        functools.partial(_mm_kernel, nk=nk, act=act, has_res=has_res),
        grid=(m // tm, n // tn, nk),
        in_specs=in_specs,
        out_specs=pl.BlockSpec((tm, tn), lambda i, j, k: (i, j)),
        out_shape=jax.ShapeDtypeStruct((m, n), out_dtype),
        scratch_shapes=[pltpu.VMEM((tm, tn), F32)] if nk > 1 else [],
        compiler_params=_cparams("parallel", "parallel", "arbitrary"),
        name=name,
    )(*args)


def _out_proj_kernel(xab_ref, xc_ref, wab_ref, wc_ref, res_ref, o_ref):
    acc = jnp.dot(xab_ref[...], wab_ref[...], preferred_element_type=F32)
    acc = acc + jnp.dot(xc_ref[...], wc_ref[...], preferred_element_type=F32)
    o_ref[...] = acc + res_ref[...]


def _out_proj(x_ab, x_c, w_ab, w_c, layer, res):
    m = x_ab.shape[0]
    n = w_ab.shape[2]
    tm, tn = _tile(m, 1024), _tile(n, 1024)
    return pl.pallas_call(
        _out_proj_kernel,
        grid=(m // tm, n // tn),
        in_specs=[pl.BlockSpec((tm, MIX_AB), lambda i, j: (i, 0)), pl.BlockSpec((tm, MIX_C), lambda i, j: (i, 0)),
                  pl.BlockSpec((None, MIX_AB, tn), lambda i, j: (layer, 0, j)),
                  pl.BlockSpec((None, MIX_C, tn), lambda i, j: (layer, 0, j)),
                  pl.BlockSpec((tm, tn), lambda i, j: (i, j))],
        out_specs=pl.BlockSpec((tm, tn), lambda i, j: (i, j)),
        out_shape=jax.ShapeDtypeStruct((m, n), F32),
        compiler_params=_cparams("parallel", "parallel"),
        name="out_proj",
    )(x_ab, x_c, w_ab, w_c, res)


def _ple_kernel(xn_ref, wg_ref, p_ref, wp_ref, h_ref, o_ref):
    z = jnp.dot(xn_ref[...], wg_ref[...], preferred_element_type=F32)
    gate = 1.0 / (1.0 + jnp.exp(-z))
    proj = jnp.dot(p_ref[...].astype(BF16), wp_ref[...], preferred_element_type=F32)
    o_ref[...] = h_ref[...] + gate * proj


def _ple(xn, w_gate, p, w_proj, layer, h):
    m, d = xn.shape
    n = w_gate.shape[2]
    pd = p.shape[1]
    tm, tn = _tile(m, 512), _tile(n, 1024)
    return pl.pallas_call(
        _ple_kernel,
        grid=(m // tm, n // tn),
        in_specs=[pl.BlockSpec((tm, d), lambda i, j: (i, 0)),
                  pl.BlockSpec((None, d, tn), lambda i, j: (layer, 0, j)),
                  pl.BlockSpec((tm, pd), lambda i, j: (i, 0)),
                  pl.BlockSpec((None, pd, tn), lambda i, j: (layer, 0, j)),
                  pl.BlockSpec((tm, tn), lambda i, j: (i, j))],
        out_specs=pl.BlockSpec((tm, tn), lambda i, j: (i, j)),
        out_shape=jax.ShapeDtypeStruct((m, n), F32),
        compiler_params=_cparams("parallel", "parallel"),
        name="ple",
    )(xn, w_gate, p, w_proj, h)


def _rope128(x, tab_ref, pat, half):
    c, sa, sb = tab_ref[3 * pat], tab_ref[3 * pat + 1], tab_ref[3 * pat + 2]
    up = pltpu.roll(x, LANES - half, axis=1)
    dn = pltpu.roll(x, half, axis=1)
    return x * c + up * sa + dn * sb


def _post_kernel(proj_ref, tab_ref, kvn_ref, wuka_ref, wukb_ref,
                 qab_ref, qidx_ref, widx_ref, qd_ref,
                 mla_ref, idx_ref, drow_ref, mla16_ref, idx16_ref, drow16_ref):
    def col(off, width=LANES):
        return proj_ref[0, :, off:off + width]

    for h in range(H_A):
        qn = col(O_QAN + h * QK_NOPE).astype(BF16)
        qab_ref[0, h, :, 0:KV_LORA] = jnp.dot(qn, wuka_ref[h], preferred_element_type=F32).astype(BF16)
    for h in range(H_B):
        qn = col(O_QBN + h * QK_NOPE).astype(BF16)
        qab_ref[0, H_A + h, :, 0:KV_LORA] = jnp.dot(qn, wukb_ref[h], preferred_element_type=F32).astype(BF16)
    for base, off, nh in ((0, O_QAR, H_A), (H_A, O_QBR, H_B)):
        for c in range(nh // 2):
            y = _rope128(col(off + c * LANES), tab_ref, 0, QK_ROPE // 2).astype(BF16)
            qab_ref[0, base + 2 * c, :, KV_LORA:MLA_ROW] = y[:, 0:QK_ROPE]
            qab_ref[0, base + 2 * c + 1, :, KV_LORA:MLA_ROW] = y[:, QK_ROPE:LANES]
    for c in range(IDX_HEADS // 2):
        y = _rope128(col(O_QIDX + c * LANES), tab_ref, 1, IDX_ROPE // 2).astype(BF16)
        qidx_ref[0, 2 * c] = y[:, 0:IDX_DIM]
        qidx_ref[0, 2 * c + 1] = y[:, IDX_DIM:LANES]
    for c in range(H_C // 2):
        y = _rope128(col(O_QD + c * LANES), tab_ref, 2, DIFF_D // 2).astype(BF16)
        qd_ref[0, 2 * c] = y[:, 0:2 * DIFF_D]
        qd_ref[0, 2 * c + 1] = y[:, 2 * DIFF_D:LANES]
    wt = jnp.transpose(col(O_W) * (IDX_HEADS ** -0.5))
    widx_ref[0] = wt[0:IDX_HEADS, :]
    ckv = col(O_CKV, KV_LORA)
    lat = ckv * lax.rsqrt(jnp.mean(ckv * ckv, axis=-1, keepdims=True) + EPS) * kvn_ref[...]
    kr = _rope128(col(O_KR), tab_ref, 0, QK_ROPE // 2)[:, 0:QK_ROPE]
    mla_ref[0, :, 0:KV_LORA] = lat
    mla_ref[0, :, KV_LORA:MLA_ROW] = kr
    mla16_ref[0, :, 0:KV_LORA] = lat.astype(BF16)
    mla16_ref[0, :, KV_LORA:MLA_ROW] = kr.astype(BF16)
    ki = _rope128(col(O_KI), tab_ref, 1, IDX_ROPE // 2)[:, 0:IDX_DIM]
    idx_ref[0] = ki
    idx16_ref[0] = ki.astype(BF16)
    dr = _rope128(col(O_DROW), tab_ref, 3, DIFF_D // 2)
    drow_ref[0] = dr
    drow16_ref[0] = dr.astype(BF16)


def _post_project(proj, tables, kv_norm, wuk_a, wuk_b, layer):
    b, t, _ = proj.shape
    tm = _tile(t, 256)
    bt = lambda w: pl.BlockSpec((1, tm, w), lambda i, j: (i, j, 0))
    hm = lambda h, w: pl.BlockSpec((1, h, tm, w), lambda i, j: (i, 0, j, 0))
    sds = jax.ShapeDtypeStruct
    return pl.pallas_call(
        _post_kernel,
        grid=(b, t // tm),
        in_specs=[bt(IN_PAD),
                  pl.BlockSpec((12, tm, LANES), lambda i, j: (0, j, 0)),
                  pl.BlockSpec((1, KV_LORA), lambda i, j: (0, 0)),
                  pl.BlockSpec((None, H_A, QK_NOPE, KV_LORA), lambda i, j: (layer, 0, 0, 0)),
                  pl.BlockSpec((None, H_B, QK_NOPE, KV_LORA), lambda i, j: (layer, 0, 0, 0))],
        out_specs=[hm(H_AB, MLA_ROW), hm(IDX_HEADS, IDX_DIM),
                   pl.BlockSpec((1, IDX_HEADS, tm), lambda i, j: (i, 0, j)),
                   hm(H_C, 2 * DIFF_D),
                   bt(MLA_ROW), bt(IDX_DIM), bt(DIFF_ROW), bt(MLA_ROW), bt(IDX_DIM), bt(DIFF_ROW)],
        out_shape=[sds((b, H_AB, t, MLA_ROW), BF16), sds((b, IDX_HEADS, t, IDX_DIM), BF16),
                   sds((b, IDX_HEADS, t), F32), sds((b, H_C, t, 2 * DIFF_D), BF16),
                   sds((b, t, MLA_ROW), F32), sds((b, t, IDX_DIM), F32), sds((b, t, DIFF_ROW), F32),
                   sds((b, t, MLA_ROW), BF16), sds((b, t, IDX_DIM), BF16), sds((b, t, DIFF_ROW), BF16)],
        compiler_params=_cparams("parallel", "parallel"),
        name="post_project",
    )(proj, tables, kv_norm.reshape(1, KV_LORA), wuk_a, wuk_b)


def _sort_key(score):
    bits = lax.bitcast_convert_type(score, jnp.int32)
    return jnp.where(bits < 0, bits ^ np.int32(0x7FFFFFFF), bits)


def _kth_largest_key(count_ge, shape, k):
    thr = jnp.full(shape, INT_MIN, jnp.int32)
    cand = jnp.zeros(shape, jnp.int32)
    thr = jnp.where(count_ge(cand) >= k, cand, thr)

    def body(i, thr):
        cand = thr | (jnp.int32(1) << (30 - i))
        return jnp.where(count_ge(cand) >= k, cand, thr)

    return lax.fori_loop(0, 31, body, thr)


def _nt_dot(a, b):
    return lax.dot_general(a, b, (((1,), (1,)), ((), ())), preferred_element_type=F32)


def _lane_fold(x, op):
    out = x[:, 0:LANES]
    for j in range(1, x.shape[1] // LANES):
        out = op(out, x[:, j * LANES:(j + 1) * LANES])
    return out


def _chunk(c, size):
    return pl.ds(pl.multiple_of(c * size, size), size)


def _sum_rows(x, chains=8):
    r, c = x.shape
    if r % (chains * SUBLANES):
        return jnp.sum(x, axis=0, keepdims=True)
    tiles = x.reshape(r // SUBLANES, SUBLANES, c)
    n = tiles.shape[0] // chains
    parts = [jnp.sum(tiles[i * n:(i + 1) * n], axis=0) for i in range(chains)]
    while len(parts) > 1:
        parts = [a + b for a, b in zip(parts[0::2], parts[1::2])]
    return jnp.sum(parts[0], axis=0, keepdims=True)


def _two_pass_softmax_matmul(n_chunks, scores_of, v_of, exp_scale, s_ref, mt_ref, lt_ref, acc_ref):
    mt_ref[...] = jnp.full(mt_ref.shape, NEG_INF, F32)

    def pass1(c, carry):
        s = scores_of(c)
        s_ref[c] = s
        mt_ref[...] = jnp.maximum(mt_ref[...], _lane_fold(s, jnp.maximum))
        return carry

    lax.fori_loop(0, n_chunks, pass1, 0)
    m = jnp.max(mt_ref[...], axis=-1, keepdims=True)
    mt_ref[...] = jnp.broadcast_to(m, mt_ref.shape)
    lt_ref[...] = jnp.zeros(lt_ref.shape, F32)
    acc_ref[...] = jnp.zeros(acc_ref.shape, F32)

    def pass2(c, carry):
        s = s_ref[c]
        mb = mt_ref[...]
        ps = [jnp.exp2((s[:, j * LANES:(j + 1) * LANES] - mb) * exp_scale) for j in range(s.shape[1] // LANES)]
        lt_ref[...] += functools.reduce(jnp.add, ps)
        p = jnp.concatenate(ps, axis=1).astype(BF16)
        acc_ref[...] += jnp.dot(p, v_of(c), preferred_element_type=F32)
        return carry

    lax.fori_loop(0, n_chunks, pass2, 0)
    return acc_ref[...], jnp.sum(lt_ref[...], axis=-1, keepdims=True)


def _prompt_latent_kernel(qab_ref, kv_ref, qidx_ref, widx_ref, kidx_ref, wuv_ref, o_ref,
                          key_ref, thr_ref, bias_ref, s_ref, mt_ref, lt_ref, acc_ref, *, topk):
    tq = qab_ref.shape[2]
    tk = s_ref.shape[2]
    n_all = key_ref.shape[0] // tk
    q0 = pl.program_id(1) * tq
    n_chunks = (q0 + tq - 1) // tk + 1
    exp_scale = (QK_NOPE + QK_ROPE) ** -0.5 * LOG2E

    key_ref[...] = jnp.full(key_ref.shape, INT_MIN, jnp.int32)
    qidx_all = qidx_ref[0].reshape(IDX_HEADS * tq, IDX_DIM)

    def causal_t(c):
        kpos = lax.broadcasted_iota(jnp.int32, (tk, tq), 0) + c * tk
        qpos = lax.broadcasted_iota(jnp.int32, (tk, tq), 1) + q0
        return kpos <= qpos

    def idx_chunk(c, carry):
        rel = jnp.maximum(_nt_dot(kidx_ref[0, _chunk(c, tk), :], qidx_all), 0.0)
        score = rel[:, 0:tq] * widx_ref[0, 0:1, :]
        for h in range(1, IDX_HEADS):
            score = score + rel[:, h * tq:(h + 1) * tq] * widx_ref[0, h:h + 1, :]
        score = score * (IDX_DIM ** -0.5)
        key_ref[_chunk(c, tk), :] = _sort_key(jnp.where(causal_t(c), score, NEG_INF))
        return carry

    lax.fori_loop(0, n_chunks, idx_chunk, 0)

    def search(limit):
        def count_ge(cand):
            return _sum_rows(jnp.where(key_ref[0:limit * tk, :] >= cand, 1.0, 0.0))

        thr_ref[...] = _kth_largest_key(count_ge, (1, tq), topk)

    half = n_all // 2
    if half * tk >= topk:
        pl.when(n_chunks <= half)(lambda: search(half))
        pl.when(n_chunks > half)(lambda: search(n_all))
    else:
        search(n_all)
    thr = thr_ref[...]

    def bias_chunk(c, carry):
        keep = jnp.logical_and(key_ref[_chunk(c, tk), :] >= thr, causal_t(c))
        bias_ref[c] = jnp.transpose(jnp.where(keep, 0.0, NEG_INF))
        return carry

    lax.fori_loop(0, n_chunks, bias_chunk, 0)

    q_all = qab_ref[0].reshape(H_AB * tq, MLA_ROW)

    def scores_of(c):
        s = _nt_dot(q_all, kv_ref[0, _chunk(c, tk), :]).reshape(H_AB, tq, tk)
        kpos = lax.broadcasted_iota(jnp.int32, (1, tq, tk), 2) + c * tk
        qpos = lax.broadcasted_iota(jnp.int32, (1, tq, tk), 1) + q0
        s_mla = jnp.where(kpos <= qpos, s[0:H_A], NEG_INF)
        s_dsa = s[H_A:H_AB] + bias_ref[c][None]
        return jnp.concatenate([s_mla, s_dsa], axis=0).reshape(H_AB * tq, tk)

    v_of = lambda c: kv_ref[0, _chunk(c, tk), 0:KV_LORA]
    acc, l = _two_pass_softmax_matmul(n_chunks, scores_of, v_of, exp_scale, s_ref, mt_ref, lt_ref, acc_ref)
    lat = (acc / l).astype(BF16)
    for h in range(H_AB):
        o_ref[0, :, h * V_HEAD:(h + 1) * V_HEAD] = jnp.dot(
            lat[h * tq:(h + 1) * tq], wuv_ref[h], preferred_element_type=F32).astype(o_ref.dtype)


def _prompt_latent_attention(qab, kv16, qidx, widx, kidx16, wuv, layer):
    b, _, t, _ = qab.shape
    tq = _tile(t, 128)
    tk = _tile(t, KEY_CHUNK)
    topk = min(IDX_TOPK, t // 4)
    assert tk >= topk and tq <= tk and tk % tq == 0
    rows = H_AB * tq
    return pl.pallas_call(
        functools.partial(_prompt_latent_kernel, topk=topk),
        grid=(b, t // tq),
        in_specs=[pl.BlockSpec((1, H_AB, tq, MLA_ROW), lambda i, j: (i, 0, j, 0)),
                  pl.BlockSpec((1, t, MLA_ROW), lambda i, j: (i, 0, 0)),
                  pl.BlockSpec((1, IDX_HEADS, tq, IDX_DIM), lambda i, j: (i, 0, j, 0)),
                  pl.BlockSpec((1, IDX_HEADS, tq), lambda i, j: (i, 0, j)),
                  pl.BlockSpec((1, t, IDX_DIM), lambda i, j: (i, 0, 0)),
                  pl.BlockSpec((None, H_AB, KV_LORA, V_HEAD), lambda i, j: (layer, 0, 0, 0))],
        out_specs=pl.BlockSpec((1, tq, MIX_AB), lambda i, j: (i, j, 0)),
        out_shape=jax.ShapeDtypeStruct((b, t, MIX_AB), BF16),
        scratch_shapes=[pltpu.VMEM((t, tq), jnp.int32), pltpu.VMEM((1, tq), jnp.int32),
                        pltpu.VMEM((t // tk, tq, tk), F32), pltpu.VMEM((t // tk, rows, tk), F32),
                        pltpu.VMEM((rows, LANES), F32), pltpu.VMEM((rows, LANES), F32),
                        pltpu.VMEM((rows, KV_LORA), F32)],
        compiler_params=_cparams("parallel", "parallel"),
        name="prompt_latent_attention",
    )(qab, kv16, qidx, widx, kidx16, wuv)


def _diff_lambda(lam_ref, lam_init):
    lv = lam_ref[...]
    a = jnp.exp(jnp.sum(lv[0:1] * lv[1:2], axis=-1, keepdims=True))
    b = jnp.exp(jnp.sum(lv[2:3] * lv[3:4], axis=-1, keepdims=True))
    return a - b + lam_init


def _split_q(q):
    lane = lax.broadcasted_iota(jnp.int32, q.shape, 1)
    zero = jnp.zeros_like(q)
    return jnp.concatenate([jnp.where(lane < DIFF_D, q, zero), jnp.where(lane >= DIFF_D, q, zero)], axis=0)


def _subln(o, g_ref, lam_init):
    y = o * lax.rsqrt(jnp.mean(o * o, axis=-1, keepdims=True) + EPS)
    return (y * g_ref[...]) * (1.0 - lam_init)


def _prompt_diff_kernel(qd_ref, dkv_ref, lam_ref, g_ref, o_ref, s_ref, mt_ref, lt_ref, acc_ref, *, lam_init):
    tq = qd_ref.shape[2]
    tk = s_ref.shape[2]
    q0 = pl.program_id(1) * tq
    n_chunks = (q0 + tq - 1) // tk + 1
    exp_scale = DIFF_D ** -0.5 * LOG2E
    lam = _diff_lambda(lam_ref, lam_init)
    qs = _split_q(qd_ref[0].reshape(H_C * tq, 2 * DIFF_D))

    def scores_of(c):
        s = _nt_dot(qs, dkv_ref[0, _chunk(c, tk), 0:2 * DIFF_D]).reshape(2 * H_C, tq, tk)
        kpos = lax.broadcasted_iota(jnp.int32, (1, tq, tk), 2) + c * tk
        qpos = lax.broadcasted_iota(jnp.int32, (1, tq, tk), 1) + q0
        return jnp.where(kpos <= qpos, s, NEG_INF).reshape(2 * H_C * tq, tk)

    v_of = lambda c: dkv_ref[0, _chunk(c, tk), :]
    acc, l = _two_pass_softmax_matmul(n_chunks, scores_of, v_of, exp_scale, s_ref, mt_ref, lt_ref, acc_ref)
    o2 = acc / l
    o = o2[0:H_C * tq] - lam * o2[H_C * tq:2 * H_C * tq]
    lane = lax.broadcasted_iota(jnp.int32, (1, DIFF_ROW), 1)
    ms = jnp.sum(jnp.where(lane >= 2 * DIFF_D, o * o, 0.0), axis=-1, keepdims=True) * (1.0 / DIFF_V)
    y = ((o * lax.rsqrt(ms + EPS)) * g_ref[...]) * (1.0 - lam_init)
    for j in range(H_C // 2):
        even = pltpu.roll(y[2 * j * tq:(2 * j + 1) * tq], 2 * DIFF_D, axis=1)
        odd = y[(2 * j + 1) * tq:(2 * j + 2) * tq]
        o_ref[0, :, j * DIFF_ROW:(j + 1) * DIFF_ROW] = jnp.where(lane < 2 * DIFF_D, even, odd).astype(o_ref.dtype)


def _prompt_diff_attention(qd, dkv16, lam_vec, subln, lam_init):
    b, _, t, _ = qd.shape
    tq = _tile(t, 64)
    tk = _tile(t, KEY_CHUNK)
    rows = 2 * H_C * tq
    return pl.pallas_call(
        functools.partial(_prompt_diff_kernel, lam_init=lam_init),
        grid=(b, t // tq),
        in_specs=[pl.BlockSpec((1, H_C, tq, 2 * DIFF_D), lambda i, j: (i, 0, j, 0)),
                  pl.BlockSpec((1, t, DIFF_ROW), lambda i, j: (i, 0, 0)),
                  pl.BlockSpec((4, DIFF_D), lambda i, j: (0, 0)),
                  pl.BlockSpec((1, DIFF_ROW), lambda i, j: (0, 0))],
        out_specs=pl.BlockSpec((1, tq, MIX_C), lambda i, j: (i, j, 0)),
        out_shape=jax.ShapeDtypeStruct((b, t, MIX_C), BF16),
        scratch_shapes=[pltpu.VMEM((t // tk, rows, tk), F32), pltpu.VMEM((rows, LANES), F32),
                        pltpu.VMEM((rows, LANES), F32), pltpu.VMEM((rows, DIFF_ROW), F32)],
        compiler_params=_cparams("parallel", "parallel"),
        name="prompt_diff_attention",
    )(qd, dkv16, lam_vec, jnp.tile(subln, 2).reshape(1, DIFF_ROW))


def _uv_kernel(lat_ref, w_ref, o_ref):
    o_ref[...] = jnp.dot(lat_ref[0, 0], w_ref[0], preferred_element_type=F32).astype(o_ref.dtype)


def _uv_project(lat, wuv, layer):
    b, h, t, c = lat.shape
    tt = _tile(t, 512)
    nt = t // tt
    return pl.pallas_call(
        _uv_kernel,
        grid=(b, nt, h),
        in_specs=[pl.BlockSpec((1, 1, tt, c), lambda i, j, k: (i, k, j, 0)),
                  pl.BlockSpec((None, 1, c, V_HEAD), lambda i, j, k: (layer, k, 0, 0))],
        out_specs=pl.BlockSpec((tt, V_HEAD), lambda i, j, k: (i * nt + j, k)),
        out_shape=jax.ShapeDtypeStruct((b * t, h * V_HEAD), BF16),
        compiler_params=_cparams("parallel", "parallel", "parallel"),
        name="uv_project",
    )(lat, wuv)


def _page_copy(cache_ref, pt_ref, buf, sem, layer, seq, slot, j):
    return pltpu.make_async_copy(cache_ref.at[layer, pt_ref[seq, j]], buf.at[slot, j], sem.at[slot])


def _start_pages(cache_ref, pt_ref, buf, sem, layer, seq, slot, n_pages):
    def body(j, c):
        _page_copy(cache_ref, pt_ref, buf, sem, layer, seq, slot, j).start()
        return c

    lax.fori_loop(0, n_pages, body, 0)


def _wait_pages(cache_ref, pt_ref, buf, sem, layer, seq, slot, n_pages):
    def body(j, c):
        _page_copy(cache_ref, pt_ref, buf, sem, layer, seq, slot, j).wait()
        return c

    lax.fori_loop(0, n_pages, body, 0)


def _paged_pipeline(cache_ref, pt_ref, buf, sem, layer, n_pages):
    b = pl.program_id(0)
    nb = pl.num_programs(0)
    slot = b % 2

    @pl.when(b == 0)
    def _():
        _start_pages(cache_ref, pt_ref, buf, sem, layer, 0, 0, n_pages)

    @pl.when(b + 1 < nb)
    def _():
        _start_pages(cache_ref, pt_ref, buf, sem, layer, b + 1, 1 - slot, n_pages)

    _wait_pages(cache_ref, pt_ref, buf, sem, layer, b, slot, n_pages)
    return slot


def _sample_index_kernel(pt_ref, cache_ref, q_ref, w_ref, knew_ref, o_ref, buf, sem, *, layer, chunk_pages):
    n_pages, page = buf.shape[1], buf.shape[3]
    slot = _paged_pipeline(cache_ref, pt_ref, buf, sem, layer, n_pages)
    q = q_ref[0]
    w = w_ref[0]
    ck = chunk_pages * page

    def chunk(c, carry):
        cols = []
        for j in range(chunk_pages):
            kt = buf[slot, c * chunk_pages + j].astype(BF16)
            rel = jnp.maximum(jnp.dot(q, kt, preferred_element_type=F32), 0.0)
            cols.append(jnp.sum(rel * w, axis=0, keepdims=True))
        o_ref[0, :, _chunk(c, ck)] = jnp.concatenate(cols, axis=1) * (IDX_DIM ** -0.5)
        return carry

    lax.fori_loop(0, n_pages // chunk_pages, chunk, 0)
    knew = knew_ref[0].astype(BF16).astype(F32)
    rel = jnp.maximum(jnp.sum(q.astype(F32) * knew, axis=-1, keepdims=True), 0.0)
    s_new = jnp.sum(rel * w, axis=0, keepdims=True) * (IDX_DIM ** -0.5)
    lane = lax.broadcasted_iota(jnp.int32, (1, LANES), 1)
    o_ref[0, :, n_pages * page:n_pages * page + LANES] = jnp.where(lane == 0, s_new, NEG_INF)


def _sample_index_scores(page_table, cache_idx_t, layer, q_idx, w_idx, k_new):
    nb, n_pages = page_table.shape
    page = cache_idx_t.shape[3]
    past = n_pages * page
    chunk_pages = _tile(n_pages, 8)
    grid_spec = pltpu.PrefetchScalarGridSpec(
        num_scalar_prefetch=1,
        grid=(nb,),
        in_specs=[pl.BlockSpec(memory_space=pl.ANY),
                  pl.BlockSpec((1, IDX_HEADS, IDX_DIM), lambda b, pt: (b, 0, 0)),
                  pl.BlockSpec((1, IDX_HEADS, 1), lambda b, pt: (b, 0, 0)),
                  pl.BlockSpec((1, 1, IDX_DIM), lambda b, pt: (b, 0, 0))],
        out_specs=pl.BlockSpec((1, 1, past + LANES), lambda b, pt: (b, 0, 0)),
        scratch_shapes=[pltpu.VMEM((2, n_pages, IDX_DIM, page), F32), pltpu.SemaphoreType.DMA((2,))],
    )
    return pl.pallas_call(
        functools.partial(_sample_index_kernel, layer=layer, chunk_pages=chunk_pages),
        grid_spec=grid_spec,
        out_shape=jax.ShapeDtypeStruct((nb, 1, past + LANES), F32),
        compiler_params=_cparams("arbitrary"),
        name="sample_index_scores",
    )(page_table, cache_idx_t, q_idx, w_idx, k_new)


def _topk_bias_kernel(s_ref, o_ref, *, topk):
    key = _sort_key(s_ref[...])
    count_ge = lambda c: jnp.sum(jnp.where(key >= c, 1.0, 0.0), axis=1, keepdims=True)
    thr = _kth_largest_key(count_ge, (key.shape[0], 1), topk)
    o_ref[...] = jnp.where(key >= thr, 0.0, NEG_INF)


def _topk_bias(scores, topk):
    r, n = scores.shape
    tr = _tile(r, 128)
    return pl.pallas_call(
        functools.partial(_topk_bias_kernel, topk=topk),
        grid=(r // tr,),
        in_specs=[pl.BlockSpec((tr, n), lambda i: (i, 0))],
        out_specs=pl.BlockSpec((tr, n), lambda i: (i, 0)),
        out_shape=jax.ShapeDtypeStruct((r, n), F32),
        compiler_params=_cparams("parallel"),
        name="topk_bias",
    )(scores)


def _sample_latent_kernel(pt_ref, cache_ref, q_ref, bias_ref, new_ref, o_ref, buf, sem, *, layer, chunk_pages):
    n_pages, page = buf.shape[1], buf.shape[3]
    hp = q_ref.shape[1]
    slot = _paged_pipeline(cache_ref, pt_ref, buf, sem, layer, n_pages)
    scale = (QK_NOPE + QK_ROPE) ** -0.5
    q = q_ref[0]
    ck = chunk_pages * page
    row = lax.broadcasted_iota(jnp.int32, (hp, 1), 0)
    is_dsa = jnp.logical_and(row >= H_A, row < H_AB)

    def masked(s, bias):
        return jnp.where(is_dsa, s + bias, s)

    def chunk(c, carry):
        m, l, acc = carry
        kts = [buf[slot, c * chunk_pages + j].astype(BF16) for j in range(chunk_pages)]
        s = jnp.concatenate([jnp.dot(q, kt, preferred_element_type=F32) for kt in kts], axis=1) * scale
        s = masked(s, bias_ref[0, :, _chunk(c, ck)])
        m = jnp.where(m > NEG_INF, m, jnp.float32(-1e30))
        m_new = jnp.maximum(m, jnp.max(s, axis=-1, keepdims=True))
        alpha = jnp.exp(m - m_new)
        p = jnp.exp(s - m_new)
        l = alpha * l + jnp.sum(p, axis=-1, keepdims=True)
        acc = alpha * acc
        for j, kt in enumerate(kts):
            acc = acc + _nt_dot(p[:, j * page:(j + 1) * page].astype(BF16), kt[0:KV_LORA, :])
        return m_new, l, acc

    new = new_ref[0].astype(BF16).astype(F32)
    lane = lax.broadcasted_iota(jnp.int32, (1, LANES), 1)
    bias_new = jnp.sum(jnp.where(lane == 0, bias_ref[0, :, n_pages * page:n_pages * page + LANES], 0.0),
                       axis=-1, keepdims=True)
    m0 = masked(jnp.sum(q.astype(F32) * new, axis=-1, keepdims=True) * scale, bias_new)
    l0 = jnp.where(m0 > NEG_INF, 1.0, 0.0)
    acc0 = l0 * new[:, 0:KV_LORA]
    m, l, acc = lax.fori_loop(0, n_pages // chunk_pages, chunk, (m0, l0, acc0))
    o_ref[0] = (acc / l).astype(o_ref.dtype)


def _sample_latent_attention(page_table, cache_mla_t, layer, q, bias, new_rows):
    nb, n_pages = page_table.shape
    page = cache_mla_t.shape[3]
    past = n_pages * page
    hp = q.shape[1]
    chunk_pages = _tile(n_pages, 8)
    grid_spec = pltpu.PrefetchScalarGridSpec(
        num_scalar_prefetch=1,
        grid=(nb,),
        in_specs=[pl.BlockSpec(memory_space=pl.ANY),
                  pl.BlockSpec((1, hp, MLA_ROW), lambda b, pt: (b, 0, 0)),
                  pl.BlockSpec((1, 1, past + LANES), lambda b, pt: (b, 0, 0)),
                  pl.BlockSpec((1, 1, MLA_ROW), lambda b, pt: (b, 0, 0))],
        out_specs=pl.BlockSpec((1, hp, KV_LORA), lambda b, pt: (b, 0, 0)),
        scratch_shapes=[pltpu.VMEM((2, n_pages, MLA_ROW, page), F32), pltpu.SemaphoreType.DMA((2,))],
    )
    return pl.pallas_call(
        functools.partial(_sample_latent_kernel, layer=layer, chunk_pages=chunk_pages),
        grid_spec=grid_spec,
        out_shape=jax.ShapeDtypeStruct((nb, hp, KV_LORA), BF16),
        compiler_params=_cparams("arbitrary"),
        name="sample_latent_attention",
    )(page_table, cache_mla_t, q, bias, new_rows)


def _sample_diff_kernel(pt_ref, cache_ref, q_ref, new_ref, lam_ref, g_ref, o_ref, buf, sem, *,
                        layer, chunk_pages, lam_init):
    n_pages, page = buf.shape[1], buf.shape[2]
    slot = _paged_pipeline(cache_ref, pt_ref, buf, sem, layer, n_pages)
    scale = DIFF_D ** -0.5
    lam = _diff_lambda(lam_ref, lam_init)
    qs = _split_q(q_ref[0])
    ck = chunk_pages * page

    new = new_ref[0].astype(BF16).astype(F32)
    m0 = jnp.sum(qs.astype(F32) * new[:, 0:2 * DIFF_D], axis=-1, keepdims=True) * scale
    l0 = jnp.ones_like(m0)
    acc0 = jnp.broadcast_to(new, (2 * H_C, DIFF_ROW))

    def chunk(c, carry):
        m, l, acc = carry
        rc = buf[slot, pl.ds(c * chunk_pages, chunk_pages)].reshape(ck, DIFF_ROW).astype(BF16)
        s = _nt_dot(qs, rc[:, 0:2 * DIFF_D]) * scale
        m_new = jnp.maximum(m, jnp.max(s, axis=-1, keepdims=True))
        alpha = jnp.exp(m - m_new)
        p = jnp.exp(s - m_new)
        l = alpha * l + jnp.sum(p, axis=-1, keepdims=True)
        acc = alpha * acc + jnp.dot(p.astype(BF16), rc, preferred_element_type=F32)
        return m_new, l, acc

    m, l, acc = lax.fori_loop(0, n_pages // chunk_pages, chunk, (m0, l0, acc0))
    o2 = acc / l
    o = (o2[0:H_C] - lam * o2[H_C:2 * H_C])[:, 2 * DIFF_D:DIFF_ROW]
    o_ref[0] = _subln(o, g_ref, lam_init).astype(o_ref.dtype)


def _sample_diff_attention(page_table, cache_diff, layer, q, new_rows, lam_vec, subln, lam_init):
    nb, n_pages = page_table.shape
    page = cache_diff.shape[2]
    chunk_pages = _tile(n_pages, 8)
    grid_spec = pltpu.PrefetchScalarGridSpec(
        num_scalar_prefetch=1,
        grid=(nb,),
        in_specs=[pl.BlockSpec(memory_space=pl.ANY),
                  pl.BlockSpec((1, H_C, 2 * DIFF_D), lambda b, pt: (b, 0, 0)),
                  pl.BlockSpec((1, 1, DIFF_ROW), lambda b, pt: (b, 0, 0)),
                  pl.BlockSpec((4, DIFF_D), lambda b, pt: (0, 0)),
                  pl.BlockSpec((1, DIFF_V), lambda b, pt: (0, 0))],
        out_specs=pl.BlockSpec((1, H_C, DIFF_V), lambda b, pt: (b, 0, 0)),
        scratch_shapes=[pltpu.VMEM((2, n_pages, page, DIFF_ROW), F32), pltpu.SemaphoreType.DMA((2,))],
    )
    return pl.pallas_call(
        functools.partial(_sample_diff_kernel, layer=layer, chunk_pages=chunk_pages, lam_init=lam_init),
        grid_spec=grid_spec,
        out_shape=jax.ShapeDtypeStruct((nb, H_C, DIFF_V), BF16),
        compiler_params=_cparams("arbitrary"),
        name="sample_diff_attention",
    )(page_table, cache_diff, q, new_rows, lam_vec, subln.reshape(1, DIFF_V))


def _lam_init(layer):
    return 0.8 - 0.6 * math.exp(-0.3 * layer)


def _layer_tail(h, mix_ab, mix_c, p, wts, layer, norm_mlp, norm_ple):
    h = _out_proj(mix_ab, mix_c, wts["w_out_ab"], wts["w_out_c"], layer, h)
    u = _matmul(_rmsnorm(h, norm_mlp, BF16), wts["w_up"], layer, act="relu2", out_dtype=BF16, name="mlp_up")
    h = _matmul(u, wts["w_down"], layer, res=h, name="mlp_down")
    return _ple(_rmsnorm(h, norm_ple, BF16), wts["w_gate"], p, wts["w_proj"], layer, h)


def _prompt_stack(x, p, wts, norm_attn, kv_norm, diff_lambda, diff_subln, norm_mlp, norm_ple, norm_final):
    b, t, d = x.shape
    m = b * t
    tables = _rope_tables(jnp.arange(t, dtype=jnp.int32))
    h = x.reshape(m, d)
    mla_rows, idx_rows, diff_rows = [], [], []
    for i in range(p.shape[0]):
        proj = _matmul(_rmsnorm(h, norm_attn[i], BF16), wts["w_in"], i, name="in_proj").reshape(b, t, IN_PAD)
        (qab, qidx, widx, qd, mla, idx, drow, mla16, idx16, drow16) = _post_project(
            proj, tables, kv_norm[i], wts["wuk_a"], wts["wuk_b"], i)
        mla_rows.append(mla)
        idx_rows.append(idx)
        diff_rows.append(drow)
        mix_ab = _prompt_latent_attention(qab, mla16, qidx, widx, idx16, wts["wuv"], i).reshape(m, MIX_AB)
        mix_c = _prompt_diff_attention(qd, drow16, diff_lambda[i], diff_subln[i], _lam_init(i)).reshape(m, MIX_C)
        h = _layer_tail(h, mix_ab, mix_c, p[i].reshape(m, -1), wts, i, norm_mlp[i], norm_ple[i])
    y = _rmsnorm(h, norm_final, F32).reshape(b, t, d)
    return y, jnp.stack(mla_rows), jnp.stack(idx_rows), jnp.stack(diff_rows)


def _sample_stack(x, p, page_table, cache_mla, cache_idx, cache_diff, wts, norm_attn, kv_norm,
                  diff_lambda, diff_subln, norm_mlp, norm_ple, norm_final):
    nb, n_new, d = x.shape
    assert n_new == 1, "the sample kernels take one new token per sequence"
    past = page_table.shape[1] * cache_mla.shape[2]
    topk = min(IDX_TOPK, (past + n_new) // 4)
    tables = _rope_tables(jnp.full((nb,), past, jnp.int32))
    cache_mla_t = jnp.swapaxes(cache_mla, 2, 3)
    cache_idx_t = jnp.swapaxes(cache_idx, 2, 3)
    h = x.reshape(nb, d)
    hp = -(-H_AB // 16) * 16
    mla_rows, idx_rows, diff_rows = [], [], []
    for i in range(p.shape[0]):
        proj = _matmul(_rmsnorm(h, norm_attn[i], BF16), wts["w_in"], i, name="in_proj").reshape(1, nb, IN_PAD)
        (qab, qidx, widx, qd, mla, idx, drow, _, _, _) = _post_project(
            proj, tables, kv_norm[i], wts["wuk_a"], wts["wuk_b"], i)
        mla_rows.append(mla.reshape(nb, 1, MLA_ROW))
        idx_rows.append(idx.reshape(nb, 1, IDX_DIM))
        diff_rows.append(drow.reshape(nb, 1, DIFF_ROW))
        q_tok = jnp.transpose(qab[0], (1, 0, 2))
        q_tok = jnp.pad(q_tok, ((0, 0), (0, hp - H_AB), (0, 0)))
        qidx_tok = jnp.transpose(qidx[0], (1, 0, 2))
        widx_tok = jnp.transpose(widx[0], (1, 0)).reshape(nb, IDX_HEADS, 1)
        qd_tok = jnp.transpose(qd[0], (1, 0, 2))
        scores = _sample_index_scores(page_table, cache_idx_t, i, qidx_tok, widx_tok, idx_rows[-1])
        bias = _topk_bias(scores.reshape(nb, past + LANES), topk).reshape(nb, 1, past + LANES)
        lat = _sample_latent_attention(page_table, cache_mla_t, i, q_tok, bias, mla_rows[-1])
        lat = jnp.transpose(lat[:, :H_AB], (1, 0, 2))[None]
        mix_ab = _uv_project(lat, wts["wuv"], i)
        mix_c = _sample_diff_attention(page_table, cache_diff, i, qd_tok, diff_rows[-1], diff_lambda[i],
                                       diff_subln[i], _lam_init(i)).reshape(nb, MIX_C)
        h = _layer_tail(h, mix_ab, mix_c, p[i].reshape(nb, -1), wts, i, norm_mlp[i], norm_ple[i])
    y = _rmsnorm(h, norm_final, F32).reshape(nb, n_new, d)
    return y, jnp.stack(mla_rows), jnp.stack(idx_rows), jnp.stack(diff_rows)


def kernel(x_prompt, x_sample, cache_mla, cache_idx, cache_diff, page_table, p_prompt, p_sample, norm_attn, w_in, kv_norm, w_uk_a, w_uv_a, w_uk_b, w_uv_b, diff_lambda, diff_subln, w_out, norm_mlp, w_up, w_down, norm_ple, w_ple_gate, w_ple_proj, norm_final):
    wts = _prep_weights(w_in, w_uk_a, w_uv_a, w_uk_b, w_uv_b, w_out, w_up, w_down, w_ple_gate, w_ple_proj)
    y_p, mla_p, idx_p, diff_p = _prompt_stack(
        x_prompt, p_prompt, wts, norm_attn, kv_norm, diff_lambda, diff_subln, norm_mlp, norm_ple, norm_final)
    y_s, mla_s, idx_s, diff_s = _sample_stack(
        x_sample, p_sample, page_table, cache_mla, cache_idx, cache_diff, wts, norm_attn, kv_norm,
        diff_lambda, diff_subln, norm_mlp, norm_ple, norm_final)
    return (y_p, y_s, mla_p, idx_p, diff_p, mla_s, idx_s, diff_s)
```

```python
import functools
import math

import jax
import jax.numpy as jnp
import numpy as np
from jax import lax
from jax.experimental import pallas as pl
from jax.experimental.pallas import tpu as pltpu

F32 = jnp.float32
BF16 = jnp.bfloat16

H_A = 12
H_B = 8
H_AB = H_A + H_B
H_C = 24
QK_NOPE = 128
QK_ROPE = 64
V_HEAD = 128
KV_LORA = 256
IDX_HEADS = 32
IDX_DIM = 64
IDX_ROPE = 32
IDX_TOPK = 256
DIFF_D = 32
DIFF_V = 2 * DIFF_D
ROPE_THETA = 10000.0
EPS = 1e-6
MLA_ROW = KV_LORA + QK_ROPE
DIFF_ROW = 2 * DIFF_D + DIFF_V
MIX_AB = H_AB * V_HEAD
MIX_C = H_C * DIFF_V

LANES = 128
SUBLANES = 8
VMEM_LIMIT = 56 * 1024 * 1024
KEY_CHUNK = 2 * LANES
LOG2E = math.log2(math.e)

_IN_COLS = (H_A * QK_NOPE, H_A * QK_ROPE, KV_LORA, QK_ROPE, H_B * QK_NOPE, H_B * QK_ROPE,
            IDX_HEADS * IDX_DIM, IDX_DIM, IDX_HEADS, H_C * 2 * DIFF_D, 2 * DIFF_D, DIFF_V)
_IN_OFF = tuple(int(v) for v in np.concatenate([[0], np.cumsum(_IN_COLS)]))

O_QAN = 0
O_QAR = O_QAN + H_A * QK_NOPE
O_CKV = O_QAR + H_A * QK_ROPE
O_QBN = O_CKV + KV_LORA
O_QBR = O_QBN + H_B * QK_NOPE
O_QIDX = O_QBR + H_B * QK_ROPE
O_QD = O_QIDX + IDX_HEADS * IDX_DIM
O_KR = O_QD + H_C * 2 * DIFF_D
O_KI = O_KR + LANES
O_DROW = O_KI + LANES
O_W = O_DROW + LANES
IN_PAD = O_W + LANES

NEG_INF = float("-inf")
INT_MIN = np.int32(-2 ** 31)


def _cparams(*sem):
    return pltpu.CompilerParams(dimension_semantics=sem, vmem_limit_bytes=VMEM_LIMIT)


def _tile(n, pref):
    if n <= pref:
        return n
    t = pref
    while n % t:
        t //= 2
    return t


def _prep_w_in(w_in):
    seg = [w_in[..., _IN_OFF[i]:_IN_OFF[i + 1]] for i in range(len(_IN_COLS))]
    (qa_n, qa_r, c_kv, k_r, qb_n, qb_r, q_idx, k_idx, w_idx, q_d, k_d, v_d) = seg
    z = lambda n: jnp.zeros(w_in.shape[:-1] + (n,), w_in.dtype)
    cols = [qa_n, qa_r, c_kv, qb_n, qb_r, q_idx, q_d,
            k_r, z(LANES - QK_ROPE), k_idx, z(LANES - IDX_DIM), k_d, v_d, w_idx, z(LANES - IDX_HEADS)]
    return jnp.concatenate(cols, axis=-1).astype(BF16)


def _prep_weights(w_in, w_uk_a, w_uv_a, w_uk_b, w_uv_b, w_out, w_up, w_down, w_ple_gate, w_ple_proj):
    wuv = jnp.concatenate([jnp.transpose(w_uv_a, (0, 2, 1, 3)), jnp.transpose(w_uv_b, (0, 2, 1, 3))], axis=1)
    return dict(
        w_in=_prep_w_in(w_in),
        wuk_a=jnp.transpose(w_uk_a, (0, 2, 3, 1)).astype(BF16),
        wuk_b=jnp.transpose(w_uk_b, (0, 2, 3, 1)).astype(BF16),
        wuv=wuv.astype(BF16),
        w_out_ab=w_out[:, :MIX_AB].astype(BF16), w_out_c=w_out[:, MIX_AB:].astype(BF16),
        w_up=w_up.astype(BF16), w_down=w_down.astype(BF16),
        w_gate=w_ple_gate.astype(BF16), w_proj=w_ple_proj.astype(BF16))


def _rope_tables(pos):
    t = pos.shape[0]

    def cs(half):
        inv = ROPE_THETA ** (-jnp.arange(half, dtype=F32) / half)
        ang = pos.astype(F32)[:, None] * inv[None, :]
        return jnp.cos(ang), jnp.sin(ang)

    c32, s32 = cs(QK_ROPE // 2)
    c16, s16 = cs(DIFF_D // 2)
    one = lambda n: jnp.ones((t, n), F32)
    zero = lambda n: jnp.zeros((t, n), F32)
    cat = lambda xs: jnp.concatenate(xs, axis=1)
    p64 = [cat([c32, c32] * 2), cat([-s32, zero(32)] * 2), cat([zero(32), s32] * 2)]
    pidx = [cat([c16, c16, one(32)] * 2), cat([-s16, zero(16), zero(32)] * 2),
            cat([zero(16), s16, zero(32)] * 2)]
    pd = [cat([c16, c16] * 4), cat([-s16, zero(16)] * 4), cat([zero(16), s16] * 4)]
    pr = [cat([c16, c16] * 2 + [one(64)]), cat([-s16, zero(16)] * 2 + [zero(64)]),
          cat([zero(16), s16] * 2 + [zero(64)])]
    return jnp.stack(p64 + pidx + pd + pr)


def _rms_kernel(x_ref, g_ref, o_ref):
    x = x_ref[...]
    y = x * lax.rsqrt(jnp.mean(x * x, axis=-1, keepdims=True) + EPS)
    o_ref[...] = (y * g_ref[...]).astype(o_ref.dtype)


def _rmsnorm(x, g, out_dtype):
    m, d = x.shape
    tm = _tile(m, 512)
    return pl.pallas_call(
        _rms_kernel,
        grid=(m // tm,),
        in_specs=[pl.BlockSpec((tm, d), lambda i: (i, 0)), pl.BlockSpec((1, d), lambda i: (0, 0))],
        out_specs=pl.BlockSpec((tm, d), lambda i: (i, 0)),
        out_shape=jax.ShapeDtypeStruct((m, d), out_dtype),
        compiler_params=_cparams("parallel"),
        name="rmsnorm",
    )(x, g.reshape(1, d))


def _mm_kernel(*refs, nk, act, has_res):
    x_ref, w_ref = refs[0], refs[1]
    res_ref = refs[2] if has_res else None
    o_ref = refs[2 + has_res]
    acc_ref = refs[3 + has_res] if nk > 1 else None
    part = jnp.dot(x_ref[...], w_ref[...], preferred_element_type=F32)

    def finish(acc):
        if act == "relu2":
            r = jnp.maximum(acc, 0.0)
            acc = r * r
        if has_res:
            acc = acc + res_ref[...]
        o_ref[...] = acc.astype(o_ref.dtype)

    if nk == 1:
        finish(part)
    else:
        k = pl.program_id(2)

        @pl.when(k == 0)
        def _():
            acc_ref[...] = part

        @pl.when(jnp.logical_and(k > 0, k < nk - 1))
        def _():
            acc_ref[...] += part

        @pl.when(k == nk - 1)
        def _():
            finish(acc_ref[...] + part)


def _matmul(x, w, layer, *, res=None, act=None, out_dtype=F32, name="matmul"):
    m, kdim = x.shape
    n = w.shape[2]
    tm, tn, tk = _tile(m, 1024), _tile(n, 1024), (kdim if kdim <= 4096 else _tile(kdim, 2048))
    nk = kdim // tk
    has_res = res is not None
    in_specs = [pl.BlockSpec((tm, tk), lambda i, j, k: (i, k)),
                pl.BlockSpec((None, tk, tn), lambda i, j, k: (layer, k, j))]
    args = [x, w]
    if has_res:
        in_specs.append(pl.BlockSpec((tm, tn), lambda i, j, k: (i, j)))
        args.append(res)
    return pl.pallas_call(
        functools.partial(_mm_kernel, nk=nk, act=act, has_res=has_res),
        grid=(m // tm, n // tn, nk),
        in_specs=in_specs,
        out_specs=pl.BlockSpec((tm, tn), lambda i, j, k: (i, j)),
        out_shape=jax.ShapeDtypeStruct((m, n), out_dtype),
        scratch_shapes=[pltpu.VMEM((tm, tn), F32)] if nk > 1 else [],
        compiler_params=_cparams("parallel", "parallel", "arbitrary"),
        name=name,
    )(*args)


def _out_proj_kernel(xab_ref, xc_ref, wab_ref, wc_ref, res_ref, o_ref):
    acc = jnp.dot(xab_ref[...], wab_ref[...], preferred_element_type=F32)
    acc = acc + jnp.dot(xc_ref[...], wc_ref[...], preferred_element_type=F32)
    o_ref[...] = acc + res_ref[...]


def _out_proj(x_ab, x_c, w_ab, w_c, layer, res):
    m = x_ab.shape[0]
    n = w_ab.shape[2]
    tm, tn = _tile(m, 1024), _tile(n, 1024)
    return pl.pallas_call(
        _out_proj_kernel,
        grid=(m // tm, n // tn),
        in_specs=[pl.BlockSpec((tm, MIX_AB), lambda i, j: (i, 0)), pl.BlockSpec((tm, MIX_C), lambda i, j: (i, 0)),
                  pl.BlockSpec((None, MIX_AB, tn), lambda i, j: (layer, 0, j)),
                  pl.BlockSpec((None, MIX_C, tn), lambda i, j: (layer, 0, j)),
                  pl.BlockSpec((tm, tn), lambda i, j: (i, j))],
        out_specs=pl.BlockSpec((tm, tn), lambda i, j: (i, j)),
        out_shape=jax.ShapeDtypeStruct((m, n), F32),
        compiler_params=_cparams("parallel", "parallel"),
        name="out_proj",
    )(x_ab, x_c, w_ab, w_c, res)


def _ple_kernel(xn_ref, wg_ref, p_ref, wp_ref, h_ref, o_ref):
    z = jnp.dot(xn_ref[...], wg_ref[...], preferred_element_type=F32)
    gate = 1.0 / (1.0 + jnp.exp(-z))
    proj = jnp.dot(p_ref[...].astype(BF16), wp_ref[...], preferred_element_type=F32)
    o_ref[...] = h_ref[...] + gate * proj


def _ple(xn, w_gate, p, w_proj, layer, h):
    m, d = xn.shape
    n = w_gate.shape[2]
    pd = p.shape[1]
    tm, tn = _tile(m, 512), _tile(n, 1024)
    return pl.pallas_call(
        _ple_kernel,
        grid=(m // tm, n // tn),
        in_specs=[pl.BlockSpec((tm, d), lambda i, j: (i, 0)),
                  pl.BlockSpec((None, d, tn), lambda i, j: (layer, 0, j)),
                  pl.BlockSpec((tm, pd), lambda i, j: (i, 0)),
                  pl.BlockSpec((None, pd, tn), lambda i, j: (layer, 0, j)),
                  pl.BlockSpec((tm, tn), lambda i, j: (i, j))],
        out_specs=pl.BlockSpec((tm, tn), lambda i, j: (i, j)),
        out_shape=jax.ShapeDtypeStruct((m, n), F32),
        compiler_params=_cparams("parallel", "parallel"),
        name="ple",
    )(xn, w_gate, p, w_proj, h)


def _rope128(x, tab_ref, pat, half):
    c, sa, sb = tab_ref[3 * pat], tab_ref[3 * pat + 1], tab_ref[3 * pat + 2]
    up = pltpu.roll(x, LANES - half, axis=1)
    dn = pltpu.roll(x, half, axis=1)
    return x * c + up * sa + dn * sb


def _post_kernel(proj_ref, tab_ref, kvn_ref, wuka_ref, wukb_ref,
                 qab_ref, qidx_ref, widx_ref, qd_ref,
                 mla_ref, idx_ref, drow_ref, mla16_ref, idx16_ref, drow16_ref):
    def col(off, width=LANES):
        return proj_ref[0, :, off:off + width]

    for h in range(H_A):
        qn = col(O_QAN + h * QK_NOPE).astype(BF16)
        qab_ref[0, h, :, 0:KV_LORA] = jnp.dot(qn, wuka_ref[h], preferred_element_type=F32).astype(BF16)
    for h in range(H_B):
        qn = col(O_QBN + h * QK_NOPE).astype(BF16)
        qab_ref[0, H_A + h, :, 0:KV_LORA] = jnp.dot(qn, wukb_ref[h], preferred_element_type=F32).astype(BF16)
    for base, off, nh in ((0, O_QAR, H_A), (H_A, O_QBR, H_B)):
        for c in range(nh // 2):
            y = _rope128(col(off + c * LANES), tab_ref, 0, QK_ROPE // 2).astype(BF16)
            qab_ref[0, base + 2 * c, :, KV_LORA:MLA_ROW] = y[:, 0:QK_ROPE]
            qab_ref[0, base + 2 * c + 1, :, KV_LORA:MLA_ROW] = y[:, QK_ROPE:LANES]
    for c in range(IDX_HEADS // 2):
        y = _rope128(col(O_QIDX + c * LANES), tab_ref, 1, IDX_ROPE // 2).astype(BF16)
        qidx_ref[0, 2 * c] = y[:, 0:IDX_DIM]
        qidx_ref[0, 2 * c + 1] = y[:, IDX_DIM:LANES]
    for c in range(H_C // 2):
        y = _rope128(col(O_QD + c * LANES), tab_ref, 2, DIFF_D // 2).astype(BF16)
        qd_ref[0, 2 * c] = y[:, 0:2 * DIFF_D]
        qd_ref[0, 2 * c + 1] = y[:, 2 * DIFF_D:LANES]
    wt = jnp.transpose(col(O_W) * (IDX_HEADS ** -0.5))
    widx_ref[0] = wt[0:IDX_HEADS, :]
    ckv = col(O_CKV, KV_LORA)
    lat = ckv * lax.rsqrt(jnp.mean(ckv * ckv, axis=-1, keepdims=True) + EPS) * kvn_ref[...]
    kr = _rope128(col(O_KR), tab_ref, 0, QK_ROPE // 2)[:, 0:QK_ROPE]
    mla_ref[0, :, 0:KV_LORA] = lat
    mla_ref[0, :, KV_LORA:MLA_ROW] = kr
    mla16_ref[0, :, 0:KV_LORA] = lat.astype(BF16)
    mla16_ref[0, :, KV_LORA:MLA_ROW] = kr.astype(BF16)
    ki = _rope128(col(O_KI), tab_ref, 1, IDX_ROPE // 2)[:, 0:IDX_DIM]
    idx_ref[0] = ki
    idx16_ref[0] = ki.astype(BF16)
    dr = _rope128(col(O_DROW), tab_ref, 3, DIFF_D // 2)
    drow_ref[0] = dr
    drow16_ref[0] = dr.astype(BF16)


def _post_project(proj, tables, kv_norm, wuk_a, wuk_b, layer):
    b, t, _ = proj.shape
    tm = _tile(t, 256)
    bt = lambda w: pl.BlockSpec((1, tm, w), lambda i, j: (i, j, 0))
    hm = lambda h, w: pl.BlockSpec((1, h, tm, w), lambda i, j: (i, 0, j, 0))
    sds = jax.ShapeDtypeStruct
    return pl.pallas_call(
        _post_kernel,
        grid=(b, t // tm),
        in_specs=[bt(IN_PAD),
                  pl.BlockSpec((12, tm, LANES), lambda i, j: (0, j, 0)),
                  pl.BlockSpec((1, KV_LORA), lambda i, j: (0, 0)),
                  pl.BlockSpec((None, H_A, QK_NOPE, KV_LORA), lambda i, j: (layer, 0, 0, 0)),
                  pl.BlockSpec((None, H_B, QK_NOPE, KV_LORA), lambda i, j: (layer, 0, 0, 0))],
        out_specs=[hm(H_AB, MLA_ROW), hm(IDX_HEADS, IDX_DIM),
                   pl.BlockSpec((1, IDX_HEADS, tm), lambda i, j: (i, 0, j)),
                   hm(H_C, 2 * DIFF_D),
                   bt(MLA_ROW), bt(IDX_DIM), bt(DIFF_ROW), bt(MLA_ROW), bt(IDX_DIM), bt(DIFF_ROW)],
        out_shape=[sds((b, H_AB, t, MLA_ROW), BF16), sds((b, IDX_HEADS, t, IDX_DIM), BF16),
                   sds((b, IDX_HEADS, t), F32), sds((b, H_C, t, 2 * DIFF_D), BF16),
                   sds((b, t, MLA_ROW), F32), sds((b, t, IDX_DIM), F32), sds((b, t, DIFF_ROW), F32),
                   sds((b, t, MLA_ROW), BF16), sds((b, t, IDX_DIM), BF16), sds((b, t, DIFF_ROW), BF16)],
        compiler_params=_cparams("parallel", "parallel"),
        name="post_project",
    )(proj, tables, kv_norm.reshape(1, KV_LORA), wuk_a, wuk_b)


def _sort_key(score):
    bits = lax.bitcast_convert_type(score, jnp.int32)
    return jnp.where(bits < 0, bits ^ np.int32(0x7FFFFFFF), bits)


def _kth_largest_key(count_ge, shape, k):
    thr = jnp.full(shape, INT_MIN, jnp.int32)
    cand = jnp.zeros(shape, jnp.int32)
    thr = jnp.where(count_ge(cand) >= k, cand, thr)

    def body(i, thr):
        cand = thr | (jnp.int32(1) << (30 - i))
        return jnp.where(count_ge(cand) >= k, cand, thr)

    return lax.fori_loop(0, 31, body, thr)


def _nt_dot(a, b):
    return lax.dot_general(a, b, (((1,), (1,)), ((), ())), preferred_element_type=F32)


def _lane_fold(x, op):
    out = x[:, 0:LANES]
    for j in range(1, x.shape[1] // LANES):
        out = op(out, x[:, j * LANES:(j + 1) * LANES])
    return out


def _chunk(c, size):
    return pl.ds(pl.multiple_of(c * size, size), size)


def _sum_rows(x, chains=8):
    r, c = x.shape
    if r % (chains * SUBLANES):
        return jnp.sum(x, axis=0, keepdims=True)
    tiles = x.reshape(r // SUBLANES, SUBLANES, c)
    n = tiles.shape[0] // chains
    parts = [jnp.sum(tiles[i * n:(i + 1) * n], axis=0) for i in range(chains)]
    while len(parts) > 1:
        parts = [a + b for a, b in zip(parts[0::2], parts[1::2])]
    return jnp.sum(parts[0], axis=0, keepdims=True)


def _two_pass_softmax_matmul(n_chunks, scores_of, v_of, exp_scale, s_ref, mt_ref, lt_ref, acc_ref):
    mt_ref[...] = jnp.full(mt_ref.shape, NEG_INF, F32)

    def pass1(c, carry):
        s = scores_of(c) * exp_scale
        s_ref[c] = s
        mt_ref[...] = jnp.maximum(mt_ref[...], _lane_fold(s, jnp.maximum))
        return carry

    lax.fori_loop(0, n_chunks, pass1, 0)
    m = jnp.max(mt_ref[...], axis=-1, keepdims=True)
    mt_ref[...] = jnp.broadcast_to(m, mt_ref.shape)
    lt_ref[...] = jnp.zeros(lt_ref.shape, F32)
    acc_ref[...] = jnp.zeros(acc_ref.shape, F32)

    def pass2(c, carry):
        s = s_ref[c]
        mb = mt_ref[...]
        ps = [jnp.exp2(s[:, j * LANES:(j + 1) * LANES] - mb) for j in range(s.shape[1] // LANES)]
        lt_ref[...] += functools.reduce(jnp.add, ps)
        p = jnp.concatenate(ps, axis=1).astype(BF16)
        acc_ref[...] += jnp.dot(p, v_of(c), preferred_element_type=F32)
        return carry

    lax.fori_loop(0, n_chunks, pass2, 0)
    return acc_ref[...], jnp.sum(lt_ref[...], axis=-1, keepdims=True)


def _prompt_latent_kernel(qab_ref, kv_ref, qidx_ref, widx_ref, kidx_ref, wuv_ref, o_ref,
                          key_ref, thr_ref, bias_ref, s_ref, mt_ref, lt_ref, acc_ref, *, topk):
    tq = qab_ref.shape[2]
    tk = s_ref.shape[2]
    n_all = key_ref.shape[0] // tk
    q0 = pl.program_id(1) * tq
    n_chunks = (q0 + tq - 1) // tk + 1
    exp_scale = (QK_NOPE + QK_ROPE) ** -0.5 * LOG2E

    key_ref[...] = jnp.full(key_ref.shape, INT_MIN, jnp.int32)
    qidx_all = qidx_ref[0].reshape(IDX_HEADS * tq, IDX_DIM)

    def causal_t(c):
        kpos = lax.broadcasted_iota(jnp.int32, (tk, tq), 0) + c * tk
        qpos = lax.broadcasted_iota(jnp.int32, (tk, tq), 1) + q0
        return kpos <= qpos

    def idx_chunk(c, carry):
        rel = jnp.maximum(_nt_dot(kidx_ref[0, _chunk(c, tk), :], qidx_all), 0.0)
        score = rel[:, 0:tq] * widx_ref[0, 0:1, :]
        for h in range(1, IDX_HEADS):
            score = score + rel[:, h * tq:(h + 1) * tq] * widx_ref[0, h:h + 1, :]
        score = score * (IDX_DIM ** -0.5)
        key_ref[_chunk(c, tk), :] = _sort_key(jnp.where(causal_t(c), score, NEG_INF))
        return carry

    lax.fori_loop(0, n_chunks, idx_chunk, 0)

    def search(limit):
        def count_ge(cand):
            return _sum_rows(jnp.where(key_ref[0:limit * tk, :] >= cand, 1.0, 0.0))

        thr_ref[...] = _kth_largest_key(count_ge, (1, tq), topk)

    half = n_all // 2
    if half * tk >= topk:
        pl.when(n_chunks <= half)(lambda: search(half))
        pl.when(n_chunks > half)(lambda: search(n_all))
    else:
        search(n_all)
    thr = thr_ref[...]

    def bias_chunk(c, carry):
        keep = jnp.logical_and(key_ref[_chunk(c, tk), :] >= thr, causal_t(c))
        bias_ref[c] = jnp.transpose(jnp.where(keep, 0.0, NEG_INF))
        return carry

    lax.fori_loop(0, n_chunks, bias_chunk, 0)

    q_all = qab_ref[0].reshape(H_AB * tq, MLA_ROW)

    def scores_of(c):
        s = _nt_dot(q_all, kv_ref[0, _chunk(c, tk), :]).reshape(H_AB, tq, tk)
        kpos = lax.broadcasted_iota(jnp.int32, (1, tq, tk), 2) + c * tk
        qpos = lax.broadcasted_iota(jnp.int32, (1, tq, tk), 1) + q0
        s_mla = jnp.where(kpos <= qpos, s[0:H_A], NEG_INF)
        s_dsa = s[H_A:H_AB] + bias_ref[c][None]
        return jnp.concatenate([s_mla, s_dsa], axis=0).reshape(H_AB * tq, tk)

    v_of = lambda c: kv_ref[0, _chunk(c, tk), 0:KV_LORA]
    acc, l = _two_pass_softmax_matmul(n_chunks, scores_of, v_of, exp_scale, s_ref, mt_ref, lt_ref, acc_ref)
    lat = (acc / l).astype(BF16)
    for h in range(H_AB):
        o_ref[0, :, h * V_HEAD:(h + 1) * V_HEAD] = jnp.dot(
            lat[h * tq:(h + 1) * tq], wuv_ref[h], preferred_element_type=F32).astype(o_ref.dtype)


def _prompt_latent_attention(qab, kv16, qidx, widx, kidx16, wuv, layer):
    b, _, t, _ = qab.shape
    tq = _tile(t, 128)
    tk = _tile(t, KEY_CHUNK)
    topk = min(IDX_TOPK, t // 4)
    assert tk >= topk and tq <= tk and tk % tq == 0
    rows = H_AB * tq
    return pl.pallas_call(
        functools.partial(_prompt_latent_kernel, topk=topk),
        grid=(b, t // tq),
        in_specs=[pl.BlockSpec((1, H_AB, tq, MLA_ROW), lambda i, j: (i, 0, j, 0)),
                  pl.BlockSpec((1, t, MLA_ROW), lambda i, j: (i, 0, 0)),
                  pl.BlockSpec((1, IDX_HEADS, tq, IDX_DIM), lambda i, j: (i, 0, j, 0)),
                  pl.BlockSpec((1, IDX_HEADS, tq), lambda i, j: (i, 0, j)),
                  pl.BlockSpec((1, t, IDX_DIM), lambda i, j: (i, 0, 0)),
                  pl.BlockSpec((None, H_AB, KV_LORA, V_HEAD), lambda i, j: (layer, 0, 0, 0))],
        out_specs=pl.BlockSpec((1, tq, MIX_AB), lambda i, j: (i, j, 0)),
        out_shape=jax.ShapeDtypeStruct((b, t, MIX_AB), BF16),
        scratch_shapes=[pltpu.VMEM((t, tq), jnp.int32), pltpu.VMEM((1, tq), jnp.int32),
                        pltpu.VMEM((t // tk, tq, tk), F32), pltpu.VMEM((t // tk, rows, tk), F32),
                        pltpu.VMEM((rows, LANES), F32), pltpu.VMEM((rows, LANES), F32),
                        pltpu.VMEM((rows, KV_LORA), F32)],
        compiler_params=_cparams("parallel", "parallel"),
        name="prompt_latent_attention",
    )(qab, kv16, qidx, widx, kidx16, wuv)


def _diff_lambda(lam_ref, lam_init):
    lv = lam_ref[...]
    a = jnp.exp(jnp.sum(lv[0:1] * lv[1:2], axis=-1, keepdims=True))
    b = jnp.exp(jnp.sum(lv[2:3] * lv[3:4], axis=-1, keepdims=True))
    return a - b + lam_init


def _split_q(q):
    lane = lax.broadcasted_iota(jnp.int32, q.shape, 1)
    zero = jnp.zeros_like(q)
    return jnp.concatenate([jnp.where(lane < DIFF_D, q, zero), jnp.where(lane >= DIFF_D, q, zero)], axis=0)


def _subln(o, g_ref, lam_init):
    y = o * lax.rsqrt(jnp.mean(o * o, axis=-1, keepdims=True) + EPS)
    return (y * g_ref[...]) * (1.0 - lam_init)


def _prompt_diff_kernel(qd_ref, dkv_ref, lam_ref, g_ref, o_ref, s_ref, mt_ref, lt_ref, acc_ref, *, lam_init):
    tq = qd_ref.shape[2]
    tk = s_ref.shape[2]
    q0 = pl.program_id(1) * tq
    n_chunks = (q0 + tq - 1) // tk + 1
    exp_scale = DIFF_D ** -0.5 * LOG2E
    lam = _diff_lambda(lam_ref, lam_init)
    qs = _split_q(qd_ref[0].reshape(H_C * tq, 2 * DIFF_D))

    def scores_of(c):
        s = _nt_dot(qs, dkv_ref[0, _chunk(c, tk), 0:2 * DIFF_D]).reshape(2 * H_C, tq, tk)
        kpos = lax.broadcasted_iota(jnp.int32, (1, tq, tk), 2) + c * tk
        qpos = lax.broadcasted_iota(jnp.int32, (1, tq, tk), 1) + q0
        return jnp.where(kpos <= qpos, s, NEG_INF).reshape(2 * H_C * tq, tk)

    v_of = lambda c: dkv_ref[0, _chunk(c, tk), :]
    acc, l = _two_pass_softmax_matmul(n_chunks, scores_of, v_of, exp_scale, s_ref, mt_ref, lt_ref, acc_ref)
    o2 = acc / l
    o = o2[0:H_C * tq] - lam * o2[H_C * tq:2 * H_C * tq]
    lane = lax.broadcasted_iota(jnp.int32, (1, DIFF_ROW), 1)
    ms = jnp.sum(jnp.where(lane >= 2 * DIFF_D, o * o, 0.0), axis=-1, keepdims=True) * (1.0 / DIFF_V)
    y = ((o * lax.rsqrt(ms + EPS)) * g_ref[...]) * (1.0 - lam_init)
    for j in range(H_C // 2):
        even = pltpu.roll(y[2 * j * tq:(2 * j + 1) * tq], 2 * DIFF_D, axis=1)
        odd = y[(2 * j + 1) * tq:(2 * j + 2) * tq]
        o_ref[0, :, j * DIFF_ROW:(j + 1) * DIFF_ROW] = jnp.where(lane < 2 * DIFF_D, even, odd).astype(o_ref.dtype)


def _prompt_diff_attention(qd, dkv16, lam_vec, subln, lam_init):
    b, _, t, _ = qd.shape
    tq = _tile(t, 64)
    tk = _tile(t, KEY_CHUNK)
    rows = 2 * H_C * tq
    return pl.pallas_call(
        functools.partial(_prompt_diff_kernel, lam_init=lam_init),
        grid=(b, t // tq),
        in_specs=[pl.BlockSpec((1, H_C, tq, 2 * DIFF_D), lambda i, j: (i, 0, j, 0)),
                  pl.BlockSpec((1, t, DIFF_ROW), lambda i, j: (i, 0, 0)),
                  pl.BlockSpec((4, DIFF_D), lambda i, j: (0, 0)),
                  pl.BlockSpec((1, DIFF_ROW), lambda i, j: (0, 0))],
        out_specs=pl.BlockSpec((1, tq, MIX_C), lambda i, j: (i, j, 0)),
        out_shape=jax.ShapeDtypeStruct((b, t, MIX_C), BF16),
        scratch_shapes=[pltpu.VMEM((t // tk, rows, tk), F32), pltpu.VMEM((rows, LANES), F32),
                        pltpu.VMEM((rows, LANES), F32), pltpu.VMEM((rows, DIFF_ROW), F32)],
        compiler_params=_cparams("parallel", "parallel"),
        name="prompt_diff_attention",
    )(qd, dkv16, lam_vec, jnp.tile(subln, 2).reshape(1, DIFF_ROW))


def _uv_kernel(lat_ref, w_ref, o_ref):
    o_ref[...] = jnp.dot(lat_ref[0, 0], w_ref[0], preferred_element_type=F32).astype(o_ref.dtype)


def _uv_project(lat, wuv, layer):
    b, h, t, c = lat.shape
    tt = _tile(t, 512)
    nt = t // tt
    return pl.pallas_call(
        _uv_kernel,
        grid=(b, nt, h),
        in_specs=[pl.BlockSpec((1, 1, tt, c), lambda i, j, k: (i, k, j, 0)),
                  pl.BlockSpec((None, 1, c, V_HEAD), lambda i, j, k: (layer, k, 0, 0))],
        out_specs=pl.BlockSpec((tt, V_HEAD), lambda i, j, k: (i * nt + j, k)),
        out_shape=jax.ShapeDtypeStruct((b * t, h * V_HEAD), BF16),
        compiler_params=_cparams("parallel", "parallel", "parallel"),
        name="uv_project",
    )(lat, wuv)


def _page_copy(cache_ref, pt_ref, buf, sem, layer, seq, slot, j):
    return pltpu.make_async_copy(cache_ref.at[layer, pt_ref[seq, j]], buf.at[slot, j], sem.at[slot])


def _start_pages(cache_ref, pt_ref, buf, sem, layer, seq, slot, n_pages):
    def body(j, c):
        _page_copy(cache_ref, pt_ref, buf, sem, layer, seq, slot, j).start()
        return c

    lax.fori_loop(0, n_pages, body, 0)


def _wait_pages(cache_ref, pt_ref, buf, sem, layer, seq, slot, n_pages):
    def body(j, c):
        _page_copy(cache_ref, pt_ref, buf, sem, layer, seq, slot, j).wait()
        return c

    lax.fori_loop(0, n_pages, body, 0)


def _paged_pipeline(cache_ref, pt_ref, buf, sem, layer, n_pages):
    b = pl.program_id(0)
    nb = pl.num_programs(0)
    slot = b % 2

    @pl.when(b == 0)
    def _():
        _start_pages(cache_ref, pt_ref, buf, sem, layer, 0, 0, n_pages)

    @pl.when(b + 1 < nb)
    def _():
        _start_pages(cache_ref, pt_ref, buf, sem, layer, b + 1, 1 - slot, n_pages)

    _wait_pages(cache_ref, pt_ref, buf, sem, layer, b, slot, n_pages)
    return slot


def _two_stream_softmax(step, n_chunks, first):
    m0, l0, acc0 = first
    empty = (jnp.full_like(m0, NEG_INF), jnp.zeros_like(l0), jnp.zeros_like(acc0))

    def pair(i, carry):
        even, odd = carry
        return step(2 * i, even), step(2 * i + 1, odd)

    (ma, la, acca), (mb, lb, accb) = lax.fori_loop(0, n_chunks // 2, pair, (first, empty))
    m = jnp.maximum(ma, mb)
    wa, wb = jnp.exp(ma - m), jnp.exp(mb - m)
    return la * wa + lb * wb, acca * wa + accb * wb


def _sample_index_kernel(pt_ref, cache_ref, q_ref, w_ref, knew_ref, o_ref, buf, sem, *, layer, chunk_pages):
    n_pages, page = buf.shape[1], buf.shape[3]
    slot = _paged_pipeline(cache_ref, pt_ref, buf, sem, layer, n_pages)
    q = q_ref[0]
    w = w_ref[0]
    ck = chunk_pages * page

    def chunk(c, carry):
        cols = []
        for j in range(chunk_pages):
            kt = buf[slot, c * chunk_pages + j].astype(BF16)
            rel = jnp.maximum(jnp.dot(q, kt, preferred_element_type=F32), 0.0)
            cols.append(jnp.sum(rel * w, axis=0, keepdims=True))
        o_ref[0, :, _chunk(c, ck)] = jnp.concatenate(cols, axis=1) * (IDX_DIM ** -0.5)
        return carry

    lax.fori_loop(0, n_pages // chunk_pages, chunk, 0)
    knew = knew_ref[0].astype(BF16).astype(F32)
    rel = jnp.maximum(jnp.sum(q.astype(F32) * knew, axis=-1, keepdims=True), 0.0)
    s_new = jnp.sum(rel * w, axis=0, keepdims=True) * (IDX_DIM ** -0.5)
    lane = lax.broadcasted_iota(jnp.int32, (1, LANES), 1)
    o_ref[0, :, n_pages * page:n_pages * page + LANES] = jnp.where(lane == 0, s_new, NEG_INF)


def _sample_index_scores(page_table, cache_idx_t, layer, q_idx, w_idx, k_new):
    nb, n_pages = page_table.shape
    page = cache_idx_t.shape[3]
    past = n_pages * page
    chunk_pages = _tile(n_pages, 8)
    grid_spec = pltpu.PrefetchScalarGridSpec(
        num_scalar_prefetch=1,
        grid=(nb,),
        in_specs=[pl.BlockSpec(memory_space=pl.ANY),
                  pl.BlockSpec((1, IDX_HEADS, IDX_DIM), lambda b, pt: (b, 0, 0)),
                  pl.BlockSpec((1, IDX_HEADS, 1), lambda b, pt: (b, 0, 0)),
                  pl.BlockSpec((1, 1, IDX_DIM), lambda b, pt: (b, 0, 0))],
        out_specs=pl.BlockSpec((1, 1, past + LANES), lambda b, pt: (b, 0, 0)),
        scratch_shapes=[pltpu.VMEM((2, n_pages, IDX_DIM, page), F32), pltpu.SemaphoreType.DMA((2,))],
    )
    return pl.pallas_call(
        functools.partial(_sample_index_kernel, layer=layer, chunk_pages=chunk_pages),
        grid_spec=grid_spec,
        out_shape=jax.ShapeDtypeStruct((nb, 1, past + LANES), F32),
        compiler_params=_cparams("arbitrary"),
        name="sample_index_scores",
    )(page_table, cache_idx_t, q_idx, w_idx, k_new)


def _topk_bias_kernel(s_ref, o_ref, *, topk):
    key = _sort_key(s_ref[...])
    count_ge = lambda c: jnp.sum(jnp.where(key >= c, 1.0, 0.0), axis=1, keepdims=True)
    thr = _kth_largest_key(count_ge, (key.shape[0], 1), topk)
    o_ref[...] = jnp.where(key >= thr, 0.0, NEG_INF)


def _topk_bias(scores, topk):
    r, n = scores.shape
    tr = _tile(r, 128)
    return pl.pallas_call(
        functools.partial(_topk_bias_kernel, topk=topk),
        grid=(r // tr,),
        in_specs=[pl.BlockSpec((tr, n), lambda i: (i, 0))],
        out_specs=pl.BlockSpec((tr, n), lambda i: (i, 0)),
        out_shape=jax.ShapeDtypeStruct((r, n), F32),
        compiler_params=_cparams("parallel"),
        name="topk_bias",
    )(scores)


def _sample_latent_kernel(pt_ref, cache_ref, q_ref, bias_ref, new_ref, o_ref, buf, sem, *, layer, chunk_pages):
    n_pages, page = buf.shape[1], buf.shape[3]
    hp = q_ref.shape[1]
    slot = _paged_pipeline(cache_ref, pt_ref, buf, sem, layer, n_pages)
    scale = (QK_NOPE + QK_ROPE) ** -0.5
    q = q_ref[0]
    ck = chunk_pages * page
    row = lax.broadcasted_iota(jnp.int32, (hp, 1), 0)
    is_dsa = jnp.logical_and(row >= H_A, row < H_AB)

    def masked(s, bias):
        return jnp.where(is_dsa, s + bias, s)

    def chunk(c, carry):
        m, l, acc = carry
        kts = [buf[slot, c * chunk_pages + j].astype(BF16) for j in range(chunk_pages)]
        s = jnp.concatenate([jnp.dot(q, kt, preferred_element_type=F32) for kt in kts], axis=1) * scale
        s = masked(s, bias_ref[0, :, _chunk(c, ck)])
        m = jnp.where(m > NEG_INF, m, jnp.float32(-1e30))
        m_new = jnp.maximum(m, jnp.max(s, axis=-1, keepdims=True))
        alpha = jnp.exp(m - m_new)
        p = jnp.exp(s - m_new)
        l = alpha * l + jnp.sum(p, axis=-1, keepdims=True)
        acc = alpha * acc
        for j, kt in enumerate(kts):
            acc = acc + _nt_dot(p[:, j * page:(j + 1) * page].astype(BF16), kt[0:KV_LORA, :])
        return m_new, l, acc

    new = new_ref[0].astype(BF16).astype(F32)
    lane = lax.broadcasted_iota(jnp.int32, (1, LANES), 1)
    bias_new = jnp.sum(jnp.where(lane == 0, bias_ref[0, :, n_pages * page:n_pages * page + LANES], 0.0),
                       axis=-1, keepdims=True)
    m0 = masked(jnp.sum(q.astype(F32) * new, axis=-1, keepdims=True) * scale, bias_new)
    l0 = jnp.where(m0 > NEG_INF, 1.0, 0.0)
    acc0 = l0 * new[:, 0:KV_LORA]
    l, acc = _two_stream_softmax(chunk, n_pages // chunk_pages, (m0, l0, acc0))
    o_ref[0] = (acc / l).astype(o_ref.dtype)


def _sample_latent_attention(page_table, cache_mla_t, layer, q, bias, new_rows):
    nb, n_pages = page_table.shape
    page = cache_mla_t.shape[3]
    past = n_pages * page
    hp = q.shape[1]
    chunk_pages = _tile(n_pages // 2, 8)
    grid_spec = pltpu.PrefetchScalarGridSpec(
        num_scalar_prefetch=1,
        grid=(nb,),
        in_specs=[pl.BlockSpec(memory_space=pl.ANY),
                  pl.BlockSpec((1, hp, MLA_ROW), lambda b, pt: (b, 0, 0)),
                  pl.BlockSpec((1, 1, past + LANES), lambda b, pt: (b, 0, 0)),
                  pl.BlockSpec((1, 1, MLA_ROW), lambda b, pt: (b, 0, 0))],
        out_specs=pl.BlockSpec((1, hp, KV_LORA), lambda b, pt: (b, 0, 0)),
        scratch_shapes=[pltpu.VMEM((2, n_pages, MLA_ROW, page), F32), pltpu.SemaphoreType.DMA((2,))],
    )
    return pl.pallas_call(
        functools.partial(_sample_latent_kernel, layer=layer, chunk_pages=chunk_pages),
        grid_spec=grid_spec,
        out_shape=jax.ShapeDtypeStruct((nb, hp, KV_LORA), BF16),
        compiler_params=_cparams("arbitrary"),
        name="sample_latent_attention",
    )(page_table, cache_mla_t, q, bias, new_rows)


def _sample_diff_kernel(pt_ref, cache_ref, q_ref, new_ref, lam_ref, g_ref, o_ref, buf, sem, *,
                        layer, chunk_pages, lam_init):
    n_pages, page = buf.shape[1], buf.shape[2]
    slot = _paged_pipeline(cache_ref, pt_ref, buf, sem, layer, n_pages)
    scale = DIFF_D ** -0.5
    lam = _diff_lambda(lam_ref, lam_init)
    qs = _split_q(q_ref[0])
    ck = chunk_pages * page

    new = new_ref[0].astype(BF16).astype(F32)
    m0 = jnp.sum(qs.astype(F32) * new[:, 0:2 * DIFF_D], axis=-1, keepdims=True) * scale
    l0 = jnp.ones_like(m0)
    acc0 = jnp.broadcast_to(new, (2 * H_C, DIFF_ROW))

    def chunk(c, carry):
        m, l, acc = carry
        rc = buf[slot, pl.ds(c * chunk_pages, chunk_pages)].reshape(ck, DIFF_ROW).astype(BF16)
        s = _nt_dot(qs, rc[:, 0:2 * DIFF_D]) * scale
        m_new = jnp.maximum(m, jnp.max(s, axis=-1, keepdims=True))
        alpha = jnp.exp(m - m_new)
        p = jnp.exp(s - m_new)
        l = alpha * l + jnp.sum(p, axis=-1, keepdims=True)
        acc = alpha * acc + jnp.dot(p.astype(BF16), rc, preferred_element_type=F32)
        return m_new, l, acc

    l, acc = _two_stream_softmax(chunk, n_pages // chunk_pages, (m0, l0, acc0))
    o2 = acc / l
    o = (o2[0:H_C] - lam * o2[H_C:2 * H_C])[:, 2 * DIFF_D:DIFF_ROW]
    o_ref[0] = _subln(o, g_ref, lam_init).astype(o_ref.dtype)


def _sample_diff_attention(page_table, cache_diff, layer, q, new_rows, lam_vec, subln, lam_init):
    nb, n_pages = page_table.shape
    page = cache_diff.shape[2]
    chunk_pages = _tile(n_pages // 2, 8)
    grid_spec = pltpu.PrefetchScalarGridSpec(
        num_scalar_prefetch=1,
        grid=(nb,),
        in_specs=[pl.BlockSpec(memory_space=pl.ANY),
                  pl.BlockSpec((1, H_C, 2 * DIFF_D), lambda b, pt: (b, 0, 0)),
                  pl.BlockSpec((1, 1, DIFF_ROW), lambda b, pt: (b, 0, 0)),
                  pl.BlockSpec((4, DIFF_D), lambda b, pt: (0, 0)),
                  pl.BlockSpec((1, DIFF_V), lambda b, pt: (0, 0))],
        out_specs=pl.BlockSpec((1, H_C, DIFF_V), lambda b, pt: (b, 0, 0)),
        scratch_shapes=[pltpu.VMEM((2, n_pages, page, DIFF_ROW), F32), pltpu.SemaphoreType.DMA((2,))],
    )
    return pl.pallas_call(
        functools.partial(_sample_diff_kernel, layer=layer, chunk_pages=chunk_pages, lam_init=lam_init),
        grid_spec=grid_spec,
        out_shape=jax.ShapeDtypeStruct((nb, H_C, DIFF_V), BF16),
        compiler_params=_cparams("arbitrary"),
        name="sample_diff_attention",
    )(page_table, cache_diff, q, new_rows, lam_vec, subln.reshape(1, DIFF_V))


def _lam_init(layer):
    return 0.8 - 0.6 * math.exp(-0.3 * layer)


def _layer_tail(h, mix_ab, mix_c, p, wts, layer, norm_mlp, norm_ple):
    h = _out_proj(mix_ab, mix_c, wts["w_out_ab"], wts["w_out_c"], layer, h)
    u = _matmul(_rmsnorm(h, norm_mlp, BF16), wts["w_up"], layer, act="relu2", out_dtype=BF16, name="mlp_up")
    h = _matmul(u, wts["w_down"], layer, res=h, name="mlp_down")
    return _ple(_rmsnorm(h, norm_ple, BF16), wts["w_gate"], p, wts["w_proj"], layer, h)


def _prompt_stack(x, p, wts, norm_attn, kv_norm, diff_lambda, diff_subln, norm_mlp, norm_ple, norm_final):
    b, t, d = x.shape
    m = b * t
    tables = _rope_tables(jnp.arange(t, dtype=jnp.int32))
    h = x.reshape(m, d)
    mla_rows, idx_rows, diff_rows = [], [], []
    for i in range(p.shape[0]):
        proj = _matmul(_rmsnorm(h, norm_attn[i], BF16), wts["w_in"], i, name="in_proj").reshape(b, t, IN_PAD)
        (qab, qidx, widx, qd, mla, idx, drow, mla16, idx16, drow16) = _post_project(
            proj, tables, kv_norm[i], wts["wuk_a"], wts["wuk_b"], i)
        mla_rows.append(mla)
        idx_rows.append(idx)
        diff_rows.append(drow)
        mix_ab = _prompt_latent_attention(qab, mla16, qidx, widx, idx16, wts["wuv"], i).reshape(m, MIX_AB)
        mix_c = _prompt_diff_attention(qd, drow16, diff_lambda[i], diff_subln[i], _lam_init(i)).reshape(m, MIX_C)
        h = _layer_tail(h, mix_ab, mix_c, p[i].reshape(m, -1), wts, i, norm_mlp[i], norm_ple[i])
    y = _rmsnorm(h, norm_final, F32).reshape(b, t, d)
    return y, jnp.stack(mla_rows), jnp.stack(idx_rows), jnp.stack(diff_rows)


def _sample_stack(x, p, page_table, cache_mla, cache_idx, cache_diff, wts, norm_attn, kv_norm,
                  diff_lambda, diff_subln, norm_mlp, norm_ple, norm_final):
    nb, n_new, d = x.shape
    assert n_new == 1, "the sample kernels take one new token per sequence"
    past = page_table.shape[1] * cache_mla.shape[2]
    topk = min(IDX_TOPK, (past + n_new) // 4)
    tables = _rope_tables(jnp.full((nb,), past, jnp.int32))
    cache_mla_t = jnp.swapaxes(cache_mla, 2, 3)
    cache_idx_t = jnp.swapaxes(cache_idx, 2, 3)
    h = x.reshape(nb, d)
    hp = -(-H_AB // 16) * 16
    mla_rows, idx_rows, diff_rows = [], [], []
    for i in range(p.shape[0]):
        proj = _matmul(_rmsnorm(h, norm_attn[i], BF16), wts["w_in"], i, name="in_proj").reshape(1, nb, IN_PAD)
        (qab, qidx, widx, qd, mla, idx, drow, _, _, _) = _post_project(
            proj, tables, kv_norm[i], wts["wuk_a"], wts["wuk_b"], i)
        mla_rows.append(mla.reshape(nb, 1, MLA_ROW))
        idx_rows.append(idx.reshape(nb, 1, IDX_DIM))
        diff_rows.append(drow.reshape(nb, 1, DIFF_ROW))
        q_tok = jnp.transpose(qab[0], (1, 0, 2))
        q_tok = jnp.pad(q_tok, ((0, 0), (0, hp - H_AB), (0, 0)))
        qidx_tok = jnp.transpose(qidx[0], (1, 0, 2))
        widx_tok = jnp.transpose(widx[0], (1, 0)).reshape(nb, IDX_HEADS, 1)
        qd_tok = jnp.transpose(qd[0], (1, 0, 2))
        scores = _sample_index_scores(page_table, cache_idx_t, i, qidx_tok, widx_tok, idx_rows[-1])
        bias = _topk_bias(scores.reshape(nb, past + LANES), topk).reshape(nb, 1, past + LANES)
        lat = _sample_latent_attention(page_table, cache_mla_t, i, q_tok, bias, mla_rows[-1])
        lat = jnp.transpose(lat[:, :H_AB], (1, 0, 2))[None]
        mix_ab = _uv_project(lat, wts["wuv"], i)
        mix_c = _sample_diff_attention(page_table, cache_diff, i, qd_tok, diff_rows[-1], diff_lambda[i],
                                       diff_subln[i], _lam_init(i)).reshape(nb, MIX_C)
        h = _layer_tail(h, mix_ab, mix_c, p[i].reshape(nb, -1), wts, i, norm_mlp[i], norm_ple[i])
    y = _rmsnorm(h, norm_final, F32).reshape(nb, n_new, d)
    return y, jnp.stack(mla_rows), jnp.stack(idx_rows), jnp.stack(diff_rows)


def kernel(x_prompt, x_sample, cache_mla, cache_idx, cache_diff, page_table, p_prompt, p_sample, norm_attn, w_in, kv_norm, w_uk_a, w_uv_a, w_uk_b, w_uv_b, diff_lambda, diff_subln, w_out, norm_mlp, w_up, w_down, norm_ple, w_ple_gate, w_ple_proj, norm_final):
    wts = _prep_weights(w_in, w_uk_a, w_uv_a, w_uk_b, w_uv_b, w_out, w_up, w_down, w_ple_gate, w_ple_proj)
    y_p, mla_p, idx_p, diff_p = _prompt_stack(
        x_prompt, p_prompt, wts, norm_attn, kv_norm, diff_lambda, diff_subln, norm_mlp, norm_ple, norm_final)
    y_s, mla_s, idx_s, diff_s = _sample_stack(
        x_sample, p_sample, page_table, cache_mla, cache_idx, cache_diff, wts, norm_attn, kv_norm,
        diff_lambda, diff_subln, norm_mlp, norm_ple, norm_final)
    return (y_p, y_s, mla_p, idx_p, diff_p, mla_s, idx_s, diff_s)
```
